```python
import math
import jax, jax.numpy as jnp
from jax import lax
import numpy as np


D_MODEL = 4096
BATCH = 16
SEQ = 256
DEPTH = 4
DEC_BATCH = 8
DEC_SEQ = 4096
PAST_LEN = 256

GRID_W = 64
HEAD_DIM = 128
Q_BLOCK = 128
N_MIXERS = 4
H_A = 16
KVH_A = 4
H_B = 16
KVH_B = 4
WINDOW = 128
H_C = 8
KVH_C = 2
H_D = 16
KVH_D = 16
NA_ROWS = 8
NA_COLS = 16
D_FF = 2048
ROPE_THETA = 10000.0
EPS = 1e-6
N_MOD = 9
NEG_INF = -1e30
F32 = jnp.float32

kernel_name = 'hybrid_diffusion_prefix_trunk_step'


def rms_norm(x, g):
    xf = x.astype(F32)
    y = xf * lax.rsqrt(jnp.mean(xf * xf, axis=-1, keepdims=True) + EPS)
    return (y * g.astype(F32)).astype(x.dtype)


def modulate(h, shift, scale):
    return h * (1.0 + scale[:, None, :]) + shift[:, None, :]


def swiglu(h, w_in, w_out):
    gate, up = jnp.split(h @ w_in, 2, axis=-1)
    return (jax.nn.silu(gate) * up) @ w_out


def half_ffn(x, shift, scale, gate, g, w_in, w_out):
    h = modulate(rms_norm(x, g), shift, scale)
    return x + 0.5 * gate[:, None, :] * swiglu(h, w_in, w_out)


def axial_rope(S):
    t = jnp.arange(S)
    row = (t // GRID_W).astype(F32)
    col = (t % GRID_W).astype(F32)
    nf = HEAD_DIM // 4
    inv = ROPE_THETA ** (-jnp.arange(nf, dtype=F32) / nf)
    ang = jnp.concatenate([row[:, None] * inv, col[:, None] * inv], axis=-1)
    return jnp.cos(ang), jnp.sin(ang)


def apply_rope(x, cos, sin):
    shp = (1, cos.shape[0]) + (1,) * (x.ndim - 3) + (cos.shape[1],)
    cs = cos.reshape(shp).astype(x.dtype)
    sn = sin.reshape(shp).astype(x.dtype)
    x1, x2 = jnp.split(x, 2, axis=-1)
    return jnp.concatenate([x1 * cs - x2 * sn, x2 * cs + x1 * sn], axis=-1)


def sweep_query_blocks(fn, q):
    B, S = q.shape[0], q.shape[1]
    nb = S // Q_BLOCK
    qb = jnp.moveaxis(q.reshape((B, nb, Q_BLOCK) + q.shape[2:]), 1, 0)
    ob = lax.map(lambda a: fn(a[0], a[1]), (jnp.arange(nb), qb))
    return jnp.moveaxis(ob, 0, 1).reshape((B, S) + ob.shape[3:])


def attend(q, k, v, mask=None, bias=None, sink=None):
    s = jnp.einsum('bqhgd,bkhd->bhgqk', q.astype(F32), k.astype(F32)) * (q.shape[-1] ** -0.5)
    if bias is not None:
        s = s + bias
    if mask is not None:
        s = jnp.where(mask, s, NEG_INF)
    m = jnp.max(s, axis=-1, keepdims=True)
    if sink is not None:
        sk = sink.astype(F32)[None, :, :, None, None]
        m = jnp.maximum(m, sk)
    e = jnp.exp(s - m)
    l = jnp.sum(e, axis=-1, keepdims=True)
    if sink is not None:
        l = l + jnp.exp(sk - m)
    l_t = jnp.transpose(l[..., 0], (0, 3, 1, 2))[..., None]
    o = jnp.einsum('bhgqk,bkhd->bqhgd', e, v.astype(F32)) / l_t
    return o.astype(v.dtype)


def gqa_proj(h, w_qkv, qn, kn, n_heads, n_kv):
    B, S, _ = h.shape
    q, k, v = jnp.split(h @ w_qkv, [n_heads * HEAD_DIM, (n_heads + n_kv) * HEAD_DIM], axis=-1)
    q = rms_norm(q.reshape(B, S, n_kv, n_heads // n_kv, HEAD_DIM), qn)
    k = rms_norm(k.reshape(B, S, n_kv, HEAD_DIM), kn)
    return q, k, v.reshape(B, S, n_kv, HEAD_DIM)


def context_gqa(h, w_qkv, w_o, qn, kn, n_heads, n_kv, sink):
    B, S, _ = h.shape
    q, k, v = gqa_proj(h, w_qkv, qn, kn, n_heads, n_kv)
    o = sweep_query_blocks(lambda j, qb: attend(qb, k, v, sink=sink), q)
    return o.reshape(B, S, -1) @ w_o, k, v


def latent_axial_gqa(h, k_ctx, v_ctx, w_qkv, w_o, qn, kn):
    B, S, _ = h.shape
    q, k, v = gqa_proj(h, w_qkv, qn, kn, H_A, KVH_A)
    cos, sin = axial_rope(S)
    q = apply_rope(q, cos, sin)
    k = apply_rope(k, cos, sin)
    k_all = jnp.concatenate([k, k_ctx.astype(k.dtype)], axis=1)
    v_all = jnp.concatenate([v, v_ctx.astype(v.dtype)], axis=1)
    o = sweep_query_blocks(lambda j, qb: attend(qb, k_all, v_all), q)
    return o.reshape(B, S, -1) @ w_o


def latent_window_gqa(h, k_ctx, v_ctx, w_qkv, w_o, qn, kn, sink):
    B, S, _ = h.shape
    L = k_ctx.shape[1]
    q, k, v = gqa_proj(h, w_qkv, qn, kn, H_B, KVH_B)
    cos, sin = axial_rope(S)
    q = apply_rope(q, cos, sin)
    k = apply_rope(k, cos, sin)
    band = Q_BLOCK + 2 * WINDOW
    pad = ((0, 0), (WINDOW, WINDOW), (0, 0), (0, 0))
    kp = jnp.pad(k, pad)
    vp = jnp.pad(v, pad)
    k_ctx = k_ctx.astype(k.dtype)
    v_ctx = v_ctx.astype(v.dtype)
    ctx_ok = jnp.ones((Q_BLOCK, L), dtype=bool)

    def block(j, qb):
        s0 = j * Q_BLOCK
        kb = lax.dynamic_slice_in_dim(kp, s0, band, axis=1)
        vb = lax.dynamic_slice_in_dim(vp, s0, band, axis=1)
        qpos = s0 + jnp.arange(Q_BLOCK)
        kpos = s0 - WINDOW + jnp.arange(band)
        ok = (kpos[None, :] >= 0) & (kpos[None, :] < S) & (jnp.abs(qpos[:, None] - kpos[None, :]) <= WINDOW)
        mask = jnp.concatenate([ok, ctx_ok], axis=1)
        return attend(qb, jnp.concatenate([kb, k_ctx], axis=1), jnp.concatenate([vb, v_ctx], axis=1), mask=mask, sink=sink)

    o = sweep_query_blocks(block, q)
    return o.reshape(B, S, -1) @ w_o


def diff_proj(h, w_qkv, qn, kn):
    B, S, _ = h.shape
    nq = H_C * 2 * HEAD_DIM
    nk = KVH_C * 2 * HEAD_DIM
    q, k, v = jnp.split(h @ w_qkv, [nq, nq + nk], axis=-1)
    q = rms_norm(q.reshape(B, S, KVH_C, H_C // KVH_C, 2, HEAD_DIM), qn)
    k = rms_norm(k.reshape(B, S, KVH_C, 2, HEAD_DIM), kn)
    return q, k, v.reshape(B, S, KVH_C, 2 * HEAD_DIM)


def diff_lambda(lq1, lk1, lq2, lk2, lam_init):
    e1 = jnp.exp(jnp.sum(lq1.astype(F32) * lk1.astype(F32)))
    e2 = jnp.exp(jnp.sum(lq2.astype(F32) * lk2.astype(F32)))
    return e1 - e2 + lam_init


def diff_attend(q, k, v, lam):
    s = jnp.einsum('bqhgid,bkhid->ibhgqk', q.astype(F32), k.astype(F32)) * (HEAD_DIM ** -0.5)
    e = jnp.exp(s - jnp.max(s, axis=-1, keepdims=True))
    l = jnp.sum(e, axis=-1)
    l_t = jnp.transpose(l, (0, 1, 4, 2, 3))[..., None]
    o = jnp.einsum('ibhgqk,bkhd->ibqhgd', e, v.astype(F32)) / l_t
    return (o[0] - lam * o[1]).astype(v.dtype)


def diff_out(o, subln, lam_init, w_o):
    B, S = o.shape[0], o.shape[1]
    o = rms_norm(o, subln) * (1.0 - lam_init)
    return o.reshape(B, S, -1) @ w_o


def context_diff(h, w_qkv, w_o, qn, kn, lam, lam_init, subln):
    q, k, v = diff_proj(h, w_qkv, qn, kn)
    o = sweep_query_blocks(lambda j, qb: diff_attend(qb, k, v, lam), q)
    return diff_out(o, subln, lam_init, w_o), k, v


def latent_diff(h, k_ctx, v_ctx, w_qkv, w_o, qn, kn, lam, lam_init, subln):
    S = h.shape[1]
    q, k, v = diff_proj(h, w_qkv, qn, kn)
    cos, sin = axial_rope(S)
    q = apply_rope(q, cos, sin)
    k = apply_rope(k, cos, sin)
    k_all = jnp.concatenate([k, k_ctx.astype(k.dtype)], axis=1)
    v_all = jnp.concatenate([v, v_ctx.astype(v.dtype)], axis=1)
    o = sweep_query_blocks(lambda j, qb: diff_attend(qb, k_all, v_all, lam), q)
    return diff_out(o, subln, lam_init, w_o)


def latent_natten(h, k_ctx, v_ctx, w_qkv, w_o, qn, kn, rpb):
    B, S, _ = h.shape
    L = k_ctx.shape[1]
    q, k, v = gqa_proj(h, w_qkv, qn, kn, H_D, KVH_D)
    rows = S // GRID_W
    kr_n = min(NA_ROWS, rows)
    q_rows = Q_BLOCK // GRID_W
    span = min(kr_n + q_rows - 1, rows)
    kg = k.reshape(B, rows, GRID_W, KVH_D, HEAD_DIM)
    vg = v.reshape(B, rows, GRID_W, KVH_D, HEAD_DIM)
    k_ctx = k_ctx.astype(k.dtype)
    v_ctx = v_ctx.astype(v.dtype)
    ctx_ok = jnp.ones((Q_BLOCK, L), dtype=bool)
    ctx_bias = jnp.zeros((H_D, Q_BLOCK, L), F32)
    rpb_f = rpb.astype(F32)
    kidx = jnp.arange(span * GRID_W)

    def block(j, qb):
        t = j * Q_BLOCK + jnp.arange(Q_BLOCK)
        r = t // GRID_W
        cl = t % GRID_W
        rs = jnp.clip(r - kr_n // 2, 0, rows - kr_n)
        cs = jnp.clip(cl - NA_COLS // 2, 0, GRID_W - NA_COLS)
        start = jnp.minimum(jnp.clip(j * q_rows - kr_n // 2, 0, rows - kr_n), rows - span)
        kb = lax.dynamic_slice_in_dim(kg, start, span, axis=1).reshape(B, span * GRID_W, KVH_D, HEAD_DIM)
        vb = lax.dynamic_slice_in_dim(vg, start, span, axis=1).reshape(B, span * GRID_W, KVH_D, HEAD_DIM)
        kr = start + kidx // GRID_W
        kc = kidx % GRID_W
        ok = ((kr[None, :] >= rs[:, None]) & (kr[None, :] < rs[:, None] + kr_n)
              & (kc[None, :] >= cs[:, None]) & (kc[None, :] < cs[:, None] + NA_COLS))
        dr = jnp.clip(kr[None, :] - r[:, None] + NA_ROWS - 1, 0, 2 * NA_ROWS - 2)
        dc = jnp.clip(kc[None, :] - cl[:, None] + NA_COLS - 1, 0, 2 * NA_COLS - 2)
        bias = jnp.concatenate([rpb_f[:, dr, dc], ctx_bias], axis=-1)
        bias = bias.reshape(KVH_D, H_D // KVH_D, Q_BLOCK, -1)
        mask = jnp.concatenate([ok, ctx_ok], axis=1)
        return attend(qb, jnp.concatenate([kb, k_ctx], axis=1), jnp.concatenate([vb, v_ctx], axis=1), mask=mask, bias=bias)

    o = sweep_query_blocks(block, q)
    return o.reshape(B, S, -1) @ w_o


def setup_inputs(seed: int = 0) -> dict:
    key = jax.random.key(seed)
    keys = iter(jax.random.split(key, 48))

    def nrm(shape, scale=1.0):
        return jax.random.normal(next(keys), shape, jnp.float32) * scale

    def gain(shape):
        return 1.0 + nrm(shape, 0.02)

    D = D_MODEL
    HD = HEAD_DIM
    inp = {}
    inp['x_prompt'] = nrm((BATCH, SEQ, D))
    inp['x_sample'] = nrm((DEC_BATCH, DEC_SEQ, D))
    inp['cache_k0'] = nrm((DEC_BATCH, PAST_LEN, KVH_A, HD))
    inp['cache_v0'] = nrm((DEC_BATCH, PAST_LEN, KVH_A, HD))
    inp['cache_k1'] = nrm((DEC_BATCH, PAST_LEN, KVH_B, HD))
    inp['cache_v1'] = nrm((DEC_BATCH, PAST_LEN, KVH_B, HD))
    inp['cache_k2'] = nrm((DEC_BATCH, PAST_LEN, KVH_C, 2, HD))
    inp['cache_v2'] = nrm((DEC_BATCH, PAST_LEN, KVH_C, 2 * HD))
    inp['cache_k3'] = nrm((DEC_BATCH, PAST_LEN, KVH_D, HD))
    inp['cache_v3'] = nrm((DEC_BATCH, PAST_LEN, KVH_D, HD))
    inp['c'] = nrm((DEC_BATCH, D))
    inp['c_ctx'] = nrm((D,))
    inp['norm_g'] = gain((DEPTH, 3, D))
    inp['w_ada'] = nrm((DEPTH, D, N_MOD * D), 0.5 * D ** -0.5)
    inp['b_ada'] = nrm((DEPTH, N_MOD * D), 0.02)
    inp['w_ffn_in'] = nrm((DEPTH, 2, D, 2 * D_FF), D ** -0.5)
    inp['w_ffn_out'] = nrm((DEPTH, 2, D_FF, D), D_FF ** -0.5)
    inp['att_wqkv'] = nrm((D, (H_A + 2 * KVH_A) * HD), D ** -0.5)
    inp['att_wo'] = nrm((H_A * HD, D), (H_A * HD) ** -0.5)
    inp['att_qn'] = gain((HD,))
    inp['att_kn'] = gain((HD,))
    inp['win_wqkv'] = nrm((D, (H_B + 2 * KVH_B) * HD), D ** -0.5)
    inp['win_wo'] = nrm((H_B * HD, D), (H_B * HD) ** -0.5)
    inp['win_qn'] = gain((HD,))
    inp['win_kn'] = gain((HD,))
    inp['win_sink'] = nrm((H_B,), 0.5)
    inp['diff_wqkv'] = nrm((D, (2 * H_C + 4 * KVH_C) * HD), D ** -0.5)
    inp['diff_wo'] = nrm((2 * H_C * HD, D), (2 * H_C * HD) ** -0.5)
    inp['diff_qn'] = gain((HD,))
    inp['diff_kn'] = gain((HD,))
    inp['diff_lq1'] = nrm((HD,), 0.1)
    inp['diff_lk1'] = nrm((HD,), 0.1)
    inp['diff_lq2'] = nrm((HD,), 0.1)
    inp['diff_lk2'] = nrm((HD,), 0.1)
    inp['diff_subln'] = gain((2 * HD,))
    inp['nat_wqkv'] = nrm((D, (H_D + 2 * KVH_D) * HD), D ** -0.5)
    inp['nat_wo'] = nrm((H_D * HD, D), (H_D * HD) ** -0.5)
    inp['nat_qn'] = gain((HD,))
    inp['nat_kn'] = gain((HD,))
    inp['nat_rpb'] = nrm((H_D, 2 * NA_ROWS - 1, 2 * NA_COLS - 1), 0.1)
    return inp


def reference(x_prompt, x_sample, cache_k0, cache_v0, cache_k1, cache_v1, cache_k2, cache_v2,
              cache_k3, cache_v3, c, c_ctx, norm_g, w_ada, b_ada, w_ffn_in, w_ffn_out,
              att_wqkv, att_wo, att_qn, att_kn,
              win_wqkv, win_wo, win_qn, win_kn, win_sink,
              diff_wqkv, diff_wo, diff_qn, diff_kn, diff_lq1, diff_lk1, diff_lq2, diff_lk2, diff_subln,
              nat_wqkv, nat_wo, nat_qn, nat_kn, nat_rpb):
    caches = [(cache_k0, cache_v0), (cache_k1, cache_v1), (cache_k2, cache_v2), (cache_k3, cache_v3)]
    sink = win_sink.reshape(KVH_B, H_B // KVH_B)
    xp, xs = x_prompt, x_sample
    new_state = []
    for i in range(DEPTH):
        kind = i % N_MIXERS
        mp = jnp.split(jax.nn.silu(c_ctx)[None, :] @ w_ada[i] + b_ada[i], N_MOD, axis=-1)
        ms = jnp.split(jax.nn.silu(c) @ w_ada[i] + b_ada[i], N_MOD, axis=-1)
        xp = half_ffn(xp, mp[0], mp[1], mp[2], norm_g[i, 0], w_ffn_in[i, 0], w_ffn_out[i, 0])
        xs = half_ffn(xs, ms[0], ms[1], ms[2], norm_g[i, 0], w_ffn_in[i, 0], w_ffn_out[i, 0])
        hp = modulate(rms_norm(xp, norm_g[i, 1]), mp[3], mp[4])
        hs = modulate(rms_norm(xs, norm_g[i, 1]), ms[3], ms[4])
        k_cache, v_cache = caches[i]
        if kind == 0:
            yp, kp, vp = context_gqa(hp, att_wqkv, att_wo, att_qn, att_kn, H_A, KVH_A, None)
            ys = latent_axial_gqa(hs, k_cache, v_cache, att_wqkv, att_wo, att_qn, att_kn)
        elif kind == 1:
            yp, kp, vp = context_gqa(hp, win_wqkv, win_wo, win_qn, win_kn, H_B, KVH_B, sink)
            ys = latent_window_gqa(hs, k_cache, v_cache, win_wqkv, win_wo, win_qn, win_kn, sink)
        elif kind == 2:
            lam_init = 0.8 - 0.6 * math.exp(-0.3 * i)
            lam = diff_lambda(diff_lq1, diff_lk1, diff_lq2, diff_lk2, lam_init)
            yp, kp, vp = context_diff(hp, diff_wqkv, diff_wo, diff_qn, diff_kn, lam, lam_init, diff_subln)
            ys = latent_diff(hs, k_cache, v_cache, diff_wqkv, diff_wo, diff_qn, diff_kn, lam, lam_init, diff_subln)
        else:
            yp, kp, vp = context_gqa(hp, nat_wqkv, nat_wo, nat_qn, nat_kn, H_D, KVH_D, None)
            ys = latent_natten(hs, k_cache, v_cache, nat_wqkv, nat_wo, nat_qn, nat_kn, nat_rpb)
        xp = xp + mp[5][:, None, :] * yp
        xs = xs + ms[5][:, None, :] * ys
        xp = half_ffn(xp, mp[6], mp[7], mp[8], norm_g[i, 2], w_ffn_in[i, 1], w_ffn_out[i, 1])
        xs = half_ffn(xs, ms[6], ms[7], ms[8], norm_g[i, 2], w_ffn_in[i, 1], w_ffn_out[i, 1])
        new_state.append(kp)
        new_state.append(vp)
    return (xp, xs, new_state[0], new_state[1], new_state[2], new_state[3],
            new_state[4], new_state[5], new_state[6], new_state[7])
```

```python
import functools
import math

import jax
import jax.numpy as jnp
import numpy as np
from jax import lax
from jax.experimental import pallas as pl
from jax.experimental.pallas import tpu as pltpu

F32 = jnp.float32
BF16 = jnp.bfloat16

HEAD_DIM = 128
GRID_W = 64
WINDOW = 128
NA_ROWS = 8
NA_COLS = 16
ROPE_THETA = 10000.0
EPS = 1e-6
N_MOD = 9
NEG_INF = -1e30
SCALE = HEAD_DIM ** -0.5

V7X_VMEM_BYTES = 64 * 1024 * 1024
VMEM_LIMIT = V7X_VMEM_BYTES - 8 * 1024 * 1024


def _params(n_grid):
    return pltpu.CompilerParams(
        dimension_semantics=("arbitrary",) * n_grid, vmem_limit_bytes=VMEM_LIMIT)


def _tile(n, pref):
    t = min(n, pref)
    while n % t:
        t //= 2
    return t


def _dot(a, b):
    return jnp.dot(a, b, preferred_element_type=F32)


def _dot_nt(a, b):
    return lax.dot_general(a, b, (((1,), (1,)), ((), ())), preferred_element_type=F32)


def _silu(x):
    return x * jax.nn.sigmoid(x)


def _rms(x, n):
    return lax.rsqrt(jnp.sum(x * x, axis=-1, keepdims=True) * (1.0 / n) + EPS)


def _adaln_kernel(c_ref, w_ref, b_ref, o_ref):
    a = _silu(c_ref[...]).astype(BF16)
    o_ref[0] = _dot(a, w_ref[0].astype(BF16)) + b_ref[0]


def adaln(cond, w_ada, b_ada):
    depth, d, n = w_ada.shape
    r = cond.shape[0]
    tn = _tile(n, 512)
    return pl.pallas_call(
        _adaln_kernel,
        grid=(depth, n // tn),
        in_specs=[
            pl.BlockSpec((r, d), lambda l, j: (0, 0)),
            pl.BlockSpec((1, d, tn), lambda l, j: (l, 0, j)),
            pl.BlockSpec((1, 1, tn), lambda l, j: (l, 0, j)),
        ],
        out_specs=pl.BlockSpec((1, r, tn), lambda l, j: (l, 0, j)),
        out_shape=jax.ShapeDtypeStruct((depth, r, n), F32),
        compiler_params=_params(2),
        name="adaln",
    )(cond, w_ada, b_ada.reshape(depth, 1, n))


def _modulated_norm(x, g, shift, scale):
    y = (x * _rms(x, x.shape[-1])) * g
    return y * (1.0 + scale) + shift


def _normmod_kernel(x_ref, g_ref, sh_ref, sc_ref, h_ref):
    h_ref[...] = _modulated_norm(x_ref[...], g_ref[...], sh_ref[0], sc_ref[0]).astype(BF16)


def normmod(x, g, shift, scale, seq):
    t, d = x.shape
    tm = _tile(seq, 512)
    per = seq // tm
    vec = pl.BlockSpec((1, 1, d), lambda i: (i // per, 0, 0))
    return pl.pallas_call(
        _normmod_kernel,
        grid=(t // tm,),
        in_specs=[pl.BlockSpec((tm, d), lambda i: (i, 0)),
                  pl.BlockSpec((1, d), lambda i: (0, 0)), vec, vec],
        out_specs=pl.BlockSpec((tm, d), lambda i: (i, 0)),
        out_shape=jax.ShapeDtypeStruct((t, d), BF16),
        compiler_params=_params(1),
        name="normmod",
    )(x, g.reshape(1, d), shift, scale)


def _ffn_in_kernel(h_ref, wg_ref, wu_ref, a_ref):
    h = h_ref[...]
    gate = _dot(h, wg_ref[...])
    up = _dot(h, wu_ref[...])
    a_ref[...] = (_silu(gate) * up).astype(BF16)


def ffn_in(h, w_in):
    t, d = h.shape
    f = w_in.shape[1] // 2
    tm = _tile(t, 1024)
    tn = _tile(f, 512)
    nj = f // tn
    return pl.pallas_call(
        _ffn_in_kernel,
        grid=(t // tm, nj),
        in_specs=[
            pl.BlockSpec((tm, d), lambda i, j: (i, 0)),
            pl.BlockSpec((d, tn), lambda i, j: (0, j)),
            pl.BlockSpec((d, tn), lambda i, j: (0, j + nj)),
        ],
        out_specs=pl.BlockSpec((tm, tn), lambda i, j: (i, j)),
        out_shape=jax.ShapeDtypeStruct((t, f), BF16),
        compiler_params=_params(2),
        name="ffn_in",
    )(h, w_in, w_in)


def _resid_kernel(a_ref, w_ref, x_ref, gate_ref, *rest, coef, with_norm):
    y = _dot(a_ref[...].astype(BF16), w_ref[...])
    xo = x_ref[...] + (coef * gate_ref[0]) * y
    if with_norm:
        g_ref, sh_ref, sc_ref, xo_ref, h_ref = rest
        xo_ref[...] = xo
        h_ref[...] = _modulated_norm(xo, g_ref[...], sh_ref[0], sc_ref[0]).astype(BF16)
    else:
        (xo_ref,) = rest
        xo_ref[...] = xo


def resid_norm(a, w, x, gate, coef, seq, norm=None):
    t, k = a.shape
    d = w.shape[1]
    tm = _tile(seq, 256)
    per = seq // tm
    vec = pl.BlockSpec((1, 1, d), lambda i: (i // per, 0, 0))
    row = pl.BlockSpec((tm, d), lambda i: (i, 0))
    in_specs = [
        pl.BlockSpec((tm, k), lambda i: (i, 0)),
        pl.BlockSpec((k, d), lambda i: (0, 0), pipeline_mode=pl.Buffered(1)),
        row, vec,
    ]
    args = [a, w, x, gate]
    if norm is not None:
        g, shift, scale = norm
        in_specs += [pl.BlockSpec((1, d), lambda i: (0, 0)), vec, vec]
        args += [g.reshape(1, d), shift, scale]
        out_specs = [row, row]
        out_shape = [jax.ShapeDtypeStruct((t, d), F32), jax.ShapeDtypeStruct((t, d), BF16)]
    else:
        out_specs = row
        out_shape = jax.ShapeDtypeStruct((t, d), F32)
    return pl.pallas_call(
        functools.partial(_resid_kernel, coef=coef, with_norm=norm is not None),
        grid=(t // tm,),
        in_specs=in_specs,
        out_specs=out_specs,
        out_shape=out_shape,
        compiler_params=_params(1),
        name="resid_norm",
    )(*args)


def _qkv_kernel(h_ref, w_ref, qn_ref, kn_ref, cos_ref, sin_ref, o_ref, *, nq, nk, rope, heads):
    j = pl.program_id(1)
    acc = _dot(h_ref[...], w_ref[...])

    def normed(gain_ref):
        for hh in range(heads):
            sl = slice(hh * HEAD_DIM, (hh + 1) * HEAD_DIM)
            xh = acc[:, sl]
            y = (xh * _rms(xh, HEAD_DIM)) * gain_ref[...]
            if rope:
                y = y * cos_ref[...] + pltpu.roll(y, HEAD_DIM // 2, 1) * sin_ref[...]
            o_ref[:, sl] = y.astype(o_ref.dtype)

    @pl.when(j < nq)
    def _():
        normed(qn_ref)

    @pl.when((j >= nq) & (j < nq + nk))
    def _():
        normed(kn_ref)

    @pl.when(j >= nq + nk)
    def _():
        o_ref[...] = acc.astype(o_ref.dtype)


def qkv_proj(h, w, qn, kn, n_q, n_k, seq, rope, out_dtype):
    t, d = h.shape
    n = w.shape[1]
    tm = _tile(seq, 1024)
    tn = _tile(math.gcd(n_q, n_k), 512)
    per = seq // tm
    cos, sin = rope if rope is not None else (jnp.zeros((seq, HEAD_DIM), F32),) * 2
    tab = pl.BlockSpec((tm, HEAD_DIM), lambda i, j: (i % per, 0))
    gain = pl.BlockSpec((1, HEAD_DIM), lambda i, j: (0, 0))
    return pl.pallas_call(
        functools.partial(_qkv_kernel, nq=n_q // tn, nk=n_k // tn, rope=rope is not None,
                          heads=tn // HEAD_DIM),
        grid=(t // tm, n // tn),
        in_specs=[
            pl.BlockSpec((tm, d), lambda i, j: (i, 0)),
            pl.BlockSpec((d, tn), lambda i, j: (0, j)),
            gain, gain, tab, tab,
        ],
        out_specs=pl.BlockSpec((tm, tn), lambda i, j: (i, j)),
        out_shape=jax.ShapeDtypeStruct((t, n), out_dtype),
        compiler_params=_params(2),
        name="qkv_proj",
    )(h, w, qn.reshape(1, HEAD_DIM), kn.reshape(1, HEAD_DIM), cos, sin)


def rope_tables(seq):
    t = jnp.arange(seq)
    row = (t // GRID_W).astype(F32)
    col = (t % GRID_W).astype(F32)
    nf = HEAD_DIM // 4
    inv = ROPE_THETA ** (-jnp.arange(nf, dtype=F32) / nf)
    ang = jnp.concatenate([row[:, None] * inv, col[:, None] * inv], axis=-1)
    cos, sin = jnp.cos(ang), jnp.sin(ang)
    return jnp.concatenate([cos, cos], axis=-1), jnp.concatenate([-sin, sin], axis=-1)


def _stack_heads(q, idx):
    parts = [q[:, i * HEAD_DIM:(i + 1) * HEAD_DIM] for i in idx]
    return parts[0] if len(parts) == 1 else jnp.concatenate(parts, axis=0)


def _softmax_step(qs, kc, vc, m_ref, l_ref, acc_ref, mask=None):
    s = _dot_nt(qs, kc) * SCALE
    if mask is not None:
        s = jnp.where(mask, s, NEG_INF)
    m_prev = m_ref[...]
    m_new = jnp.maximum(m_prev, jnp.max(s, axis=-1, keepdims=True))
    alpha = jnp.exp(m_prev - m_new)
    p = jnp.exp(s - m_new[:, :1])
    l_ref[...] = alpha * l_ref[...] + jnp.sum(p, axis=-1, keepdims=True)
    reps = acc_ref.shape[-1] // HEAD_DIM
    a_full = alpha if reps == 1 else jnp.concatenate([alpha] * reps, axis=1)
    acc_ref[...] = a_full * acc_ref[...] + _dot(p.astype(BF16), vc)
    m_ref[...] = m_new


def _attn_kernel(*refs, groups, tq, tk, seq, has_cache, has_sink, window):
    refs = list(refs)
    q_ref, k_ref, v_ref = refs[:3]
    pos = 3
    if has_cache:
        ck_ref, cv_ref = refs[pos:pos + 2]
        pos += 2
    if has_sink:
        sink_ref = refs[pos]
        pos += 1
    o_ref, m_ref, l_ref, acc_ref = refs[pos:]
    rows = groups * tq

    qs = _stack_heads(q_ref[...].astype(BF16), range(groups))
    if has_sink:
        sk = sink_ref[0]
        parts = [jnp.broadcast_to(sk[g:g + 1, :], (tq, HEAD_DIM)) for g in range(groups)]
        m_ref[...] = parts[0] if groups == 1 else jnp.concatenate(parts, axis=0)
        l_ref[...] = jnp.ones((rows, HEAD_DIM), F32)
    else:
        m_ref[...] = jnp.full((rows, HEAD_DIM), NEG_INF, F32)
        l_ref[...] = jnp.zeros((rows, HEAD_DIM), F32)
    acc_ref[...] = jnp.zeros(acc_ref.shape, F32)

    if window:
        band = tq + 2 * WINDOW
        q0 = pl.program_id(2) * tq
        start = pl.multiple_of(jnp.clip(q0 - WINDOW, 0, seq - band), WINDOW)
        kc = k_ref[pl.ds(start, band), :].astype(BF16)
        vc = v_ref[pl.ds(start, band), :].astype(BF16)
        qpos = q0 + lax.rem(lax.broadcasted_iota(jnp.int32, (rows, band), 0), tq)
        kpos = start + lax.broadcasted_iota(jnp.int32, (rows, band), 1)
        mask = jnp.abs(qpos - kpos) <= WINDOW
        _softmax_step(qs, kc, vc, m_ref, l_ref, acc_ref, mask)
    else:
        def body(c, carry):
            off = pl.multiple_of(c * tk, tk)
            _softmax_step(qs, k_ref[pl.ds(off, tk), :].astype(BF16),
                          v_ref[pl.ds(off, tk), :].astype(BF16), m_ref, l_ref, acc_ref)
            return carry
        lax.fori_loop(0, seq // tk, body, 0)
    if has_cache:
        _softmax_step(qs, ck_ref[...].astype(BF16), cv_ref[...].astype(BF16), m_ref, l_ref, acc_ref)

    o = acc_ref[...] / l_ref[...]
    for g in range(groups):
        o_ref[:, g * HEAD_DIM:(g + 1) * HEAD_DIM] = o[g * tq:(g + 1) * tq].astype(o_ref.dtype)


def attention(qkv, n_heads, n_kv, cache=None, sink=None, window=False):
    b, seq, _ = qkv.shape
    groups = n_heads // n_kv
    tq = _tile(seq, 256)
    tk = _tile(seq, 512)
    if window:
        assert seq >= tq + 2 * WINDOW and tq % WINDOW == 0
    gw = groups * HEAD_DIM
    in_specs = [
        pl.BlockSpec((None, tq, gw), lambda bi, h, qi: (bi, qi, h)),
        pl.BlockSpec((None, seq, HEAD_DIM), lambda bi, h, qi: (bi, 0, n_heads + h)),
        pl.BlockSpec((None, seq, HEAD_DIM), lambda bi, h, qi: (bi, 0, n_heads + n_kv + h)),
    ]
    args = [qkv, qkv, qkv]
    if cache is not None:
        ck, cv = cache
        past = ck.shape[1]
        spec = pl.BlockSpec((None, past, HEAD_DIM), lambda bi, h, qi: (bi, 0, h))
        in_specs += [spec, spec]
        args += [ck.reshape(b, past, n_kv * HEAD_DIM), cv.reshape(b, past, n_kv * HEAD_DIM)]
    if sink is not None:
        in_specs.append(pl.BlockSpec((1, groups, HEAD_DIM), lambda bi, h, qi: (h, 0, 0)))
        args.append(jnp.broadcast_to(sink.astype(F32).reshape(n_kv, groups, 1), (n_kv, groups, HEAD_DIM)))
    rows = groups * tq
    return pl.pallas_call(
        functools.partial(_attn_kernel, groups=groups, tq=tq, tk=tk, seq=seq,
                          has_cache=cache is not None, has_sink=sink is not None, window=window),
        grid=(b, n_kv, seq // tq),
        in_specs=in_specs,
        out_specs=pl.BlockSpec((None, tq, gw), lambda bi, h, qi: (bi, qi, h)),
        out_shape=jax.ShapeDtypeStruct((b, seq, n_heads * HEAD_DIM), BF16),
        scratch_shapes=[pltpu.VMEM((rows, HEAD_DIM), F32), pltpu.VMEM((rows, HEAD_DIM), F32),
                        pltpu.VMEM((rows, HEAD_DIM), F32)],
        compiler_params=_params(3),
        name="attention",
    )(*args)


def _diff_kernel(*refs, groups, tq, tk, seq, has_cache, lam_init):
    refs = list(refs)
    q_ref, k_ref, v_ref = refs[:3]
    pos = 3
    if has_cache:
        ck_ref, cv_ref = refs[pos:pos + 2]
        pos += 2
    lq1_ref, lk1_ref, lq2_ref, lk2_ref, sub_ref = refs[pos:pos + 5]
    o_ref, m_ref, l_ref, acc_ref = refs[pos + 5:]
    rows = groups * tq
    dv = 2 * HEAD_DIM

    q = q_ref[...].astype(BF16)
    qs = [_stack_heads(q, [2 * g + i for g in range(groups)]) for i in range(2)]
    m_ref[...] = jnp.full(m_ref.shape, NEG_INF, F32)
    l_ref[...] = jnp.zeros(l_ref.shape, F32)
    acc_ref[...] = jnp.zeros(acc_ref.shape, F32)

    def both_maps(kc, vc):
        for i in range(2):
            _softmax_step(qs[i], kc[:, i * HEAD_DIM:(i + 1) * HEAD_DIM], vc,
                          m_ref.at[i], l_ref.at[i], acc_ref.at[i])

    def body(c, carry):
        off = pl.multiple_of(c * tk, tk)
        both_maps(k_ref[pl.ds(off, tk), :].astype(BF16), v_ref[pl.ds(off, tk), :].astype(BF16))
        return carry
    lax.fori_loop(0, seq // tk, body, 0)
    if has_cache:
        both_maps(ck_ref[...].astype(BF16), cv_ref[...].astype(BF16))

    e1 = jnp.exp(jnp.sum(lq1_ref[...] * lk1_ref[...], axis=-1, keepdims=True))
    e2 = jnp.exp(jnp.sum(lq2_ref[...] * lk2_ref[...], axis=-1, keepdims=True))
    lam = e1 - e2 + lam_init
    l0 = l_ref[0]
    l1 = l_ref[1]
    o0 = acc_ref[0] / jnp.concatenate([l0, l0], axis=1)
    o1 = acc_ref[1] / jnp.concatenate([l1, l1], axis=1)
    o = o0 - lam * o1
    o = ((o * _rms(o, dv)) * sub_ref[...]) * (1.0 - lam_init)
    for g in range(groups):
        o_ref[:, g * dv:(g + 1) * dv] = o[g * tq:(g + 1) * tq].astype(o_ref.dtype)


def diff_attention(qkv, n_heads, n_kv, lam_vecs, subln, lam_init, cache=None):
    b, seq, _ = qkv.shape
    groups = n_heads // n_kv
    dv = 2 * HEAD_DIM
    tq = _tile(seq, 256)
    tk = _tile(seq, 512)
    gw = groups * dv
    q_blocks = n_heads * dv // gw
    kv0 = n_heads * dv // dv
    in_specs = [
        pl.BlockSpec((None, tq, gw), lambda bi, h, qi: (bi, qi, h)),
        pl.BlockSpec((None, seq, dv), lambda bi, h, qi: (bi, 0, kv0 + h)),
        pl.BlockSpec((None, seq, dv), lambda bi, h, qi: (bi, 0, kv0 + n_kv + h)),
    ]
    del q_blocks
    args = [qkv, qkv, qkv]
    if cache is not None:
        ck, cv = cache
        past = ck.shape[1]
        spec = pl.BlockSpec((None, past, dv), lambda bi, h, qi: (bi, 0, h))
        in_specs += [spec, spec]
        args += [ck.reshape(b, past, n_kv * dv), cv.reshape(b, past, n_kv * dv)]
    vec = pl.BlockSpec((1, HEAD_DIM), lambda bi, h, qi: (0, 0))
    in_specs += [vec] * 4 + [pl.BlockSpec((1, dv), lambda bi, h, qi: (0, 0))]
    args += [v.astype(F32).reshape(1, HEAD_DIM) for v in lam_vecs] + [subln.astype(F32).reshape(1, dv)]
    rows = groups * tq
    return pl.pallas_call(
        functools.partial(_diff_kernel, groups=groups, tq=tq, tk=tk, seq=seq,
                          has_cache=cache is not None, lam_init=lam_init),
        grid=(b, n_kv, seq // tq),
        in_specs=in_specs,
        out_specs=pl.BlockSpec((None, tq, gw), lambda bi, h, qi: (bi, qi, h)),
        out_shape=jax.ShapeDtypeStruct((b, seq, n_heads * dv), BF16),
        scratch_shapes=[pltpu.VMEM((2, rows, HEAD_DIM), F32), pltpu.VMEM((2, rows, HEAD_DIM), F32),
                        pltpu.VMEM((2, rows, dv), F32)],
        compiler_params=_params(3),
        name="diff_attention",
    )(*args)


NAT_QB = 128


def _nat_geometry(seq):
    rows = seq // GRID_W
    kr_n = min(NA_ROWS, rows)
    q_rows = NAT_QB // GRID_W
    span = min(kr_n + q_rows - 1, rows)
    return rows, kr_n, q_rows, span


def _nat_block_pattern(j, seq):
    rows, kr_n, q_rows, span = _nat_geometry(seq)
    t = j * NAT_QB + np.arange(NAT_QB)
    r, cl = t // GRID_W, t % GRID_W
    rs = np.clip(r - kr_n // 2, 0, rows - kr_n)
    cs = np.clip(cl - NA_COLS // 2, 0, GRID_W - NA_COLS)
    start = min(int(np.clip(j * q_rows - kr_n // 2, 0, rows - kr_n)), rows - span)
    kidx = np.arange(span * GRID_W)
    kr, kc = start + kidx // GRID_W, kidx % GRID_W
    ok = ((kr[None, :] >= rs[:, None]) & (kr[None, :] < rs[:, None] + kr_n)
          & (kc[None, :] >= cs[:, None]) & (kc[None, :] < cs[:, None] + NA_COLS))
    dr = np.clip(kr[None, :] - r[:, None] + NA_ROWS - 1, 0, 2 * NA_ROWS - 2)
    dc = np.clip(kc[None, :] - cl[:, None] + NA_COLS - 1, 0, 2 * NA_COLS - 2)
    return start, ok, dr, dc


def _nat_class(j, nb):
    return jnp.minimum(j, 2) + jnp.maximum(j - (nb - 3), 0)


def nat_bias_tables(rpb, seq):
    nb = seq // NAT_QB
    assert nb >= 5
    reps = [0, 1, 2, nb - 2, nb - 1]
    pats = [_nat_block_pattern(j, seq) for j in reps]
    for j in range(nb):
        cls = min(j, 2) + max(j - (nb - 3), 0)
        start, ok, dr, dc = _nat_block_pattern(j, seq)
        assert (ok == pats[cls][1]).all() and (np.where(ok, dr, 0) == np.where(ok, pats[cls][2], 0)).all()
        assert (np.where(ok, dc, 0) == np.where(ok, pats[cls][3], 0)).all()
    rpb_f = rpb.astype(F32)
    tabs = [jnp.where(ok[None], rpb_f[:, dr, dc], NEG_INF) for _, ok, dr, dc in pats]
    return jnp.stack(tabs, axis=0)


def _nat_kernel(q_ref, k_ref, v_ref, ck_ref, cv_ref, t_ref, o_ref, *, seq):
    rows, kr_n, q_rows, span = _nat_geometry(seq)
    nb = seq // NAT_QB
    band = span * GRID_W
    kctx = ck_ref[...].astype(BF16)
    vctx = cv_ref[...].astype(BF16)

    def body(j, carry):
        q0 = pl.multiple_of(j * NAT_QB, NAT_QB)
        q = q_ref[pl.ds(q0, NAT_QB), :]
        start = jnp.minimum(jnp.clip(j * q_rows - kr_n // 2, 0, rows - kr_n), rows - span)
        k0 = pl.multiple_of(start * GRID_W, GRID_W)
        kb = k_ref[pl.ds(k0, band), :]
        vb = v_ref[pl.ds(k0, band), :]
        s1 = _dot_nt(q, kb) * SCALE + t_ref[_nat_class(j, nb), 0]
        s2 = _dot_nt(q, kctx) * SCALE
        m = jnp.maximum(jnp.max(s1, axis=-1, keepdims=True), jnp.max(s2, axis=-1, keepdims=True))
        p1 = jnp.exp(s1 - m)
        p2 = jnp.exp(s2 - m)
        l = jnp.sum(p1, axis=-1, keepdims=True) + jnp.sum(p2, axis=-1, keepdims=True)
        o = (_dot(p1.astype(BF16), vb) + _dot(p2.astype(BF16), vctx)) / l
        o_ref[pl.ds(q0, NAT_QB), :] = o.astype(o_ref.dtype)
        return carry
    lax.fori_loop(0, nb, body, 0)


def nat_attention(qkv, n_heads, cache, tables):
    b, seq, _ = qkv.shape
    ck, cv = cache
    past = ck.shape[1]
    ncls, _, _, band = tables.shape
    blk = lambda off: pl.BlockSpec((None, seq, HEAD_DIM), lambda bi, h: (bi, 0, off + h))
    cspec = pl.BlockSpec((None, past, HEAD_DIM), lambda bi, h: (bi, 0, h))
    return pl.pallas_call(
        functools.partial(_nat_kernel, seq=seq),
        grid=(b, n_heads),
        in_specs=[blk(0), blk(n_heads), blk(2 * n_heads), cspec, cspec,
                  pl.BlockSpec((ncls, 1, NAT_QB, band), lambda bi, h: (0, h, 0, 0))],
        out_specs=pl.BlockSpec((None, seq, HEAD_DIM), lambda bi, h: (bi, 0, h)),
        out_shape=jax.ShapeDtypeStruct((b, seq, n_heads * HEAD_DIM), BF16),
        compiler_params=_params(2),
        name="nat_attention",
    )(qkv, qkv, qkv, ck.reshape(b, past, n_heads * HEAD_DIM), cv.reshape(b, past, n_heads * HEAD_DIM), tables)


def kernel(x_prompt, x_sample, cache_k0, cache_v0, cache_k1, cache_v1, cache_k2, cache_v2, cache_k3, cache_v3, c, c_ctx, norm_g, w_ada, b_ada, w_ffn_in, w_ffn_out, att_wqkv, att_wo, att_qn, att_kn, win_wqkv, win_wo, win_qn, win_kn, win_sink, diff_wqkv, diff_wo, diff_qn, diff_kn, diff_lq1, diff_lk1, diff_lq2, diff_lk2, diff_subln, nat_wqkv, nat_wo, nat_qn, nat_kn, nat_rpb):
    depth = w_ada.shape[0]
    d = x_prompt.shape[-1]
    bp, sp, _ = x_prompt.shape
    bs, ss, _ = x_sample.shape
    caches = [(cache_k0, cache_v0), (cache_k1, cache_v1), (cache_k2, cache_v2), (cache_k3, cache_v3)]
    kv_heads = [cache_k0.shape[2], cache_k1.shape[2], cache_k2.shape[2], cache_k3.shape[2]]
    wqkv = [w.astype(BF16) for w in (att_wqkv, win_wqkv, diff_wqkv, nat_wqkv)]
    wo = [w.astype(BF16) for w in (att_wo, win_wo, diff_wo, nat_wo)]
    q_cols = [w.shape[0] for w in wo]
    qn = [att_qn, win_qn, diff_qn, nat_qn]
    kn = [att_kn, win_kn, diff_kn, nat_kn]
    w_in = w_ffn_in.astype(BF16)
    w_out = w_ffn_out.astype(BF16)
    lam_vecs = (diff_lq1, diff_lk1, diff_lq2, diff_lk2)

    rows = 8 * (-(-(1 + bs) // 8))
    cond = jnp.zeros((rows, d), F32).at[0].set(c_ctx).at[1:1 + bs].set(c)
    mods = adaln(cond, w_ada, b_ada).reshape(depth, rows, N_MOD, d)

    rope = rope_tables(ss)
    tables = nat_bias_tables(nat_rpb, ss)

    def run(x3, latent):
        b, seq, _ = x3.shape
        mseq = seq if latent else b * seq

        def mod(l, chunk):
            r = mods[l, 1:1 + bs, chunk] if latent else mods[l, 0:1, chunk]
            return r[:, None, :]

        x = x3.reshape(b * seq, d)
        h = normmod(x, norm_g[0, 0], mod(0, 0), mod(0, 1), mseq)
        state = []
        for l in range(depth):
            kind = l % 4
            n_kv = kv_heads[kind]
            a = ffn_in(h, w_in[l, 0])
            x, h = resid_norm(a, w_out[l, 0], x, mod(l, 2), 0.5, mseq,
                              norm=(norm_g[l, 1], mod(l, 3), mod(l, 4)))
            kv_cols = (wqkv[kind].shape[1] - q_cols[kind]) // 2
            qkv = qkv_proj(h, wqkv[kind], qn[kind], kn[kind], q_cols[kind], kv_cols, seq,
                           rope if (latent and kind != 3) else None, BF16 if latent else F32)
            qkv = qkv.reshape(b, seq, -1)
            cache = caches[kind] if latent else None
            if kind == 2:
                lam_init = 0.8 - 0.6 * math.exp(-0.3 * l)
                n_heads = q_cols[kind] // (2 * HEAD_DIM)
                o = diff_attention(qkv, n_heads, n_kv, lam_vecs, diff_subln, lam_init, cache)
            elif kind == 3 and latent:
                n_heads = q_cols[kind] // HEAD_DIM
                o = nat_attention(qkv, n_heads, cache, tables)
            else:
                n_heads = q_cols[kind] // HEAD_DIM
                o = attention(qkv, n_heads, n_kv, cache,
                              sink=win_sink if kind == 1 else None, window=(kind == 1 and latent))
            if not latent:
                nq = q_cols[kind]
                state.append(qkv[:, :, nq:nq + kv_cols])
                state.append(qkv[:, :, nq + kv_cols:])
            x, h = resid_norm(o.reshape(b * seq, -1), wo[kind], x, mod(l, 5), 1.0, mseq,
                              norm=(norm_g[l, 2], mod(l, 6), mod(l, 7)))
            a = ffn_in(h, w_in[l, 1])
            if l + 1 < depth:
                x, h = resid_norm(a, w_out[l, 1], x, mod(l, 8), 0.5, mseq,
                                  norm=(norm_g[l + 1, 0], mod(l + 1, 0), mod(l + 1, 1)))
            else:
                x = resid_norm(a, w_out[l, 1], x, mod(l, 8), 0.5, mseq)
        return x.reshape(b, seq, d), state

    yp, st = run(x_prompt, False)
    ys, _ = run(x_sample, True)
    hd = HEAD_DIM
    new_state = [
        st[0].reshape(bp, sp, kv_heads[0], hd), st[1].reshape(bp, sp, kv_heads[0], hd),
        st[2].reshape(bp, sp, kv_heads[1], hd), st[3].reshape(bp, sp, kv_heads[1], hd),
        st[4].reshape(bp, sp, kv_heads[2], 2, hd), st[5].reshape(bp, sp, kv_heads[2], 2 * hd),
        st[6].reshape(bp, sp, kv_heads[3], hd), st[7].reshape(bp, sp, kv_heads[3], hd),
    ]
    return (yp, ys, *new_state)
```

```python
import functools
import math

import jax
import jax.numpy as jnp
import numpy as np
from jax import lax
from jax.experimental import pallas as pl
from jax.experimental.pallas import tpu as pltpu

F32 = jnp.float32
BF16 = jnp.bfloat16

HEAD_DIM = 128
LANES = 128
GRID_W = 64
WINDOW = 128
NA_ROWS = 8
NA_COLS = 16
ROPE_THETA = 10000.0
EPS = 1e-6
N_MOD = 9
NEG_INF = -1e30
LOG2E = math.log2(math.e)
QK_SCALE2 = HEAD_DIM ** -0.5 * LOG2E

V7X_VMEM_BYTES = 64 * 1024 * 1024
VMEM_LIMIT = V7X_VMEM_BYTES - 8 * 1024 * 1024


def _params(n_grid):
    return pltpu.CompilerParams(
        dimension_semantics=("arbitrary",) * n_grid, vmem_limit_bytes=VMEM_LIMIT)


def _tile(n, pref):
    t = min(n, pref)
    while n % t:
        t //= 2
    return t


def _dot(a, b):
    return jnp.dot(a, b, preferred_element_type=F32)


def _dot_nt(a, b):
    return lax.dot_general(a, b, (((1,), (1,)), ((), ())), preferred_element_type=F32)


def _silu(x):
    return x * jax.nn.sigmoid(x)


def _rms(x, n):
    return lax.rsqrt(jnp.sum(x * x, axis=-1, keepdims=True) * (1.0 / n) + EPS)


def _adaln_kernel(c_ref, w_ref, b_ref, o_ref):
    a = _silu(c_ref[...]).astype(BF16)
    o_ref[0] = _dot(a, w_ref[0].astype(BF16)) + b_ref[0]


def adaln(cond, w_ada, b_ada):
    depth, d, n = w_ada.shape
    r = cond.shape[0]
    tn = _tile(n, 512)
    return pl.pallas_call(
        _adaln_kernel,
        grid=(depth, n // tn),
        in_specs=[
            pl.BlockSpec((r, d), lambda l, j: (0, 0)),
            pl.BlockSpec((1, d, tn), lambda l, j: (l, 0, j)),
            pl.BlockSpec((1, 1, tn), lambda l, j: (l, 0, j)),
        ],
        out_specs=pl.BlockSpec((1, r, tn), lambda l, j: (l, 0, j)),
        out_shape=jax.ShapeDtypeStruct((depth, r, n), F32),
        compiler_params=_params(2),
        name="adaln",
    )(cond, w_ada, b_ada.reshape(depth, 1, n))


def _modulated_norm(x, g, shift, scale):
    y = (x * _rms(x, x.shape[-1])) * g
    return y * (1.0 + scale) + shift


def _normmod_kernel(x_ref, g_ref, sh_ref, sc_ref, h_ref):
    h_ref[...] = _modulated_norm(x_ref[...], g_ref[...], sh_ref[0], sc_ref[0]).astype(BF16)


def normmod(x, g, shift, scale, seq):
    t, d = x.shape
    tm = _tile(seq, 512)
    per = seq // tm
    vec = pl.BlockSpec((1, 1, d), lambda i: (i // per, 0, 0))
    return pl.pallas_call(
        _normmod_kernel,
        grid=(t // tm,),
        in_specs=[pl.BlockSpec((tm, d), lambda i: (i, 0)),
                  pl.BlockSpec((1, d), lambda i: (0, 0)), vec, vec],
        out_specs=pl.BlockSpec((tm, d), lambda i: (i, 0)),
        out_shape=jax.ShapeDtypeStruct((t, d), BF16),
        compiler_params=_params(1),
        name="normmod",
    )(x, g.reshape(1, d), shift, scale)


def _ffn_in_kernel(h_ref, wg_ref, wu_ref, a_ref):
    h = h_ref[...]
    gate = _dot(h, wg_ref[...])
    up = _dot(h, wu_ref[...])
    a_ref[...] = (_silu(gate) * up).astype(BF16)


def ffn_in(h, w_in):
    t, d = h.shape
    f = w_in.shape[1] // 2
    tm = _tile(t, 1024)
    tn = _tile(f, 512)
    nj = f // tn
    return pl.pallas_call(
        _ffn_in_kernel,
        grid=(t // tm, nj),
        in_specs=[
            pl.BlockSpec((tm, d), lambda i, j: (i, 0)),
            pl.BlockSpec((d, tn), lambda i, j: (0, j)),
            pl.BlockSpec((d, tn), lambda i, j: (0, j + nj)),
        ],
        out_specs=pl.BlockSpec((tm, tn), lambda i, j: (i, j)),
        out_shape=jax.ShapeDtypeStruct((t, f), BF16),
        compiler_params=_params(2),
        name="ffn_in",
    )(h, w_in, w_in)


RESID_ROW_SPLIT = 2


def _by_parity(step, fn, bufs):
    parity = lax.rem(step, 2)

    @pl.when(parity == 0)
    def _():
        fn(bufs[0], bufs[1])

    @pl.when(parity == 1)
    def _():
        fn(bufs[1], bufs[0])


def _resid_kernel(a_ref, w_ref, x_ref, gate_ref, *rest, coef, with_norm):
    tm = a_ref.shape[0]
    split = RESID_ROW_SPLIT if tm % (16 * RESID_ROW_SPLIT) == 0 else 1
    sub = tm // split
    gate = coef * gate_ref[0]
    for r in range(split):
        rs = slice(r * sub, (r + 1) * sub)
        y = _dot(a_ref[rs, :].astype(BF16), w_ref[...])
        xo = x_ref[rs, :] + gate * y
        if with_norm:
            g_ref, sh_ref, sc_ref, xo_ref, h_ref = rest
            xo_ref[rs, :] = xo
            h_ref[rs, :] = _modulated_norm(xo, g_ref[...], sh_ref[0], sc_ref[0]).astype(BF16)
        else:
            (xo_ref,) = rest
            xo_ref[rs, :] = xo


def resid_norm(a, w, x, gate, coef, seq, norm=None):
    t, k = a.shape
    d = w.shape[1]
    tm = _tile(seq, 256)
    per = seq // tm
    vec = pl.BlockSpec((1, 1, d), lambda i: (i // per, 0, 0))
    row = pl.BlockSpec((tm, d), lambda i: (i, 0))
    in_specs = [
        pl.BlockSpec((tm, k), lambda i: (i, 0)),
        pl.BlockSpec((k, d), lambda i: (0, 0), pipeline_mode=pl.Buffered(1)),
        row, vec,
    ]
    args = [a, w, x, gate]
    if norm is not None:
        g, shift, scale = norm
        in_specs += [pl.BlockSpec((1, d), lambda i: (0, 0)), vec, vec]
        args += [g.reshape(1, d), shift, scale]
        out_specs = [row, row]
        out_shape = [jax.ShapeDtypeStruct((t, d), F32), jax.ShapeDtypeStruct((t, d), BF16)]
    else:
        out_specs = row
        out_shape = jax.ShapeDtypeStruct((t, d), F32)
    return pl.pallas_call(
        functools.partial(_resid_kernel, coef=coef, with_norm=norm is not None),
        grid=(t // tm,),
        in_specs=in_specs,
        out_specs=out_specs,
        out_shape=out_shape,
        compiler_params=_params(1),
        name="resid_norm",
    )(*args)


def _qkv_kernel(h_ref, w_ref, gain_ref, *rest, n_norm, n_col, rope, heads):
    acc_refs = rest[-2:]
    rest = rest[:-2]
    step = pl.program_id(0)
    is_value = lax.rem(jnp.maximum(step - 1, 0), n_col) >= n_norm
    if rope:
        cos_ref, sin_ref, o_ref = rest
    else:
        (o_ref,) = rest

    @pl.when(step == 0)
    def _():
        acc_refs[1][...] = jnp.zeros(acc_refs[1].shape, F32)

    def body(acc_w, acc_r):
        for hh in range(heads):
            sl = slice(hh * HEAD_DIM, (hh + 1) * HEAD_DIM)
            xh = acc_r[:, sl]
            y = (xh * jnp.where(is_value, 1.0, _rms(xh, HEAD_DIM))) * gain_ref[:, sl]
            if rope:
                y = y * cos_ref[...] + pltpu.roll(y, HEAD_DIM // 2, 1) * sin_ref[...]
            o_ref[:, sl] = y.astype(o_ref.dtype)
        acc_w[...] = _dot(h_ref[...], w_ref[...])

    _by_parity(step, body, acc_refs)


def qkv_proj(h, w, qn, kn, n_q, n_k, seq, rope, out_dtype):
    t, d = h.shape
    n = w.shape[1]
    tm = _tile(seq, 1024)
    tn = _tile(math.gcd(n_q, n_k), 512)
    per = seq // tm
    n_norm = (n_q + n_k) // tn
    n_col = n // tn
    n_tiles = (t // tm) * n_col
    gains = jnp.concatenate([jnp.tile(qn.astype(F32), n_q // HEAD_DIM), jnp.tile(kn.astype(F32), n_k // HEAD_DIM),
                             jnp.ones((n - n_q - n_k,), F32)]).reshape(1, n)
    cur = lambda s: jnp.minimum(s, n_tiles - 1)
    lag = lambda s: jnp.maximum(s - 1, 0)
    in_specs = [
        pl.BlockSpec((tm, d), lambda s: (cur(s) // n_col, 0)),
        pl.BlockSpec((d, tn), lambda s: (0, cur(s) % n_col)),
        pl.BlockSpec((1, tn), lambda s: (0, lag(s) % n_col)),
    ]
    args = [h, w, gains]
    if rope is not None:
        cos, sin = rope
        tab = pl.BlockSpec((None, tm, HEAD_DIM),
                           lambda s: (jnp.where(lag(s) % n_col >= n_norm, 1, 0), (lag(s) // n_col) % per, 0))
        in_specs += [tab, tab]
        args += [jnp.stack([cos, jnp.ones_like(cos)]), jnp.stack([sin, jnp.zeros_like(sin)])]
    return pl.pallas_call(
        functools.partial(_qkv_kernel, n_norm=n_norm, n_col=n_col, rope=rope is not None,
                          heads=tn // HEAD_DIM),
        grid=(n_tiles + 1,),
        in_specs=in_specs,
        out_specs=pl.BlockSpec((tm, tn), lambda s: (lag(s) // n_col, lag(s) % n_col)),
        out_shape=jax.ShapeDtypeStruct((t, n), out_dtype),
        scratch_shapes=[pltpu.VMEM((tm, tn), F32), pltpu.VMEM((tm, tn), F32)],
        compiler_params=_params(1),
        name="qkv_proj",
    )(*args)


def rope_tables(seq):
    t = jnp.arange(seq)
    row = (t // GRID_W).astype(F32)
    col = (t % GRID_W).astype(F32)
    nf = HEAD_DIM // 4
    inv = ROPE_THETA ** (-jnp.arange(nf, dtype=F32) / nf)
    ang = jnp.concatenate([row[:, None] * inv, col[:, None] * inv], axis=-1)
    cos, sin = jnp.cos(ang), jnp.sin(ang)
    return jnp.concatenate([cos, cos], axis=-1), jnp.concatenate([-sin, sin], axis=-1)


ATTN_TQ = 256
ATTN_TK = 1024


def _stack_heads(q, idx):
    parts = [q[:, i * HEAD_DIM:(i + 1) * HEAD_DIM] for i in idx]
    return parts[0] if len(parts) == 1 else jnp.concatenate(parts, axis=0)


def _lane_slabs(x):
    return [x[:, i * LANES:(i + 1) * LANES] for i in range(x.shape[1] // LANES)]


def _tile_lanes(x, width):
    n = width // LANES
    return x if n == 1 else jnp.concatenate([x] * n, axis=1)


def _tile_rows(x, n):
    return x if n == 1 else jnp.concatenate([x] * n, axis=0)


def _logits_pass(qs, kc, s_out, m_ref, bias=None):
    s = _dot_nt(qs, kc) * QK_SCALE2
    if bias is not None:
        s = s + bias
    s_out[...] = s
    m_ref[...] = jnp.maximum(m_ref[...], functools.reduce(jnp.maximum, _lane_slabs(s)))


def _finish_max(m_ref):
    m_ref[...] = jnp.broadcast_to(jnp.max(m_ref[...], axis=-1, keepdims=True), m_ref.shape)


def _value_pass(s_in, vc, m_ref, l_ref, acc_ref):
    s = s_in[...]
    p = jnp.exp2(s - _tile_lanes(m_ref[...], s.shape[1]))
    l_ref[...] += functools.reduce(jnp.add, _lane_slabs(p))
    acc_ref[...] += _dot(p.astype(BF16), vc)


def _row_sum(l_ref):
    return jnp.sum(l_ref[...], axis=-1, keepdims=True)


def _sink_rows(sink_ref, groups, tq):
    sk = sink_ref[0]
    parts = [jnp.broadcast_to(sk[g:g + 1, :], (tq, LANES)) for g in range(groups)]
    return parts[0] if groups == 1 else jnp.concatenate(parts, axis=0)


def _attn_kernel(*refs, groups, tq, tk, seq, has_cache, has_sink):
    refs = list(refs)
    q_ref, k_ref, v_ref = refs[:3]
    pos = 3
    if has_cache:
        ck_ref, cv_ref = refs[pos:pos + 2]
        pos += 2
    if has_sink:
        sink_ref = refs[pos]
        pos += 1
    o_ref, s_ref = refs[pos:pos + 2]
    pos += 2
    if has_cache:
        sc_ref = refs[pos]
        pos += 1
    m_ref, l_ref, acc_ref = refs[pos:]
    rows = groups * tq
    n_chunks = seq // tk

    qs = _stack_heads(q_ref[...].astype(BF16), range(groups))
    if has_sink:
        sink2 = _sink_rows(sink_ref, groups, tq)
        m_ref[...] = sink2
    else:
        m_ref[...] = jnp.full((rows, LANES), NEG_INF, F32)
    l_ref[...] = jnp.zeros((rows, LANES), F32)
    acc_ref[...] = jnp.zeros(acc_ref.shape, F32)

    def p1(c, carry):
        off = pl.multiple_of(c * tk, tk)
        _logits_pass(qs, k_ref[pl.ds(off, tk), :].astype(BF16), s_ref.at[c], m_ref)
        return carry
    lax.fori_loop(0, n_chunks, p1, 0)
    if has_cache:
        _logits_pass(qs, ck_ref[...].astype(BF16), sc_ref, m_ref)
    _finish_max(m_ref)

    def p2(c, carry):
        off = pl.multiple_of(c * tk, tk)
        _value_pass(s_ref.at[c], v_ref[pl.ds(off, tk), :].astype(BF16), m_ref, l_ref, acc_ref)
        return carry
    lax.fori_loop(0, n_chunks, p2, 0)
    if has_cache:
        _value_pass(sc_ref, cv_ref[...].astype(BF16), m_ref, l_ref, acc_ref)

    l = _row_sum(l_ref)
    if has_sink:
        l = l + jnp.exp2(sink2 - m_ref[...])[:, :1]
    o = acc_ref[...] / l
    for g in range(groups):
        o_ref[:, g * HEAD_DIM:(g + 1) * HEAD_DIM] = o[g * tq:(g + 1) * tq].astype(o_ref.dtype)


def _sink_arg(sink, n_kv, groups):
    sink2 = sink.astype(F32).reshape(n_kv, groups, 1) * LOG2E
    return jnp.broadcast_to(sink2, (n_kv, groups, LANES))


def attention(qkv, n_heads, n_kv, cache=None, sink=None):
    b, seq, _ = qkv.shape
    groups = n_heads // n_kv
    tq = _tile(seq, ATTN_TQ)
    tk = _tile(seq, ATTN_TK)
    gw = groups * HEAD_DIM
    rows = groups * tq
    in_specs = [
        pl.BlockSpec((None, tq, gw), lambda bi, h, qi: (bi, qi, h)),
        pl.BlockSpec((None, seq, HEAD_DIM), lambda bi, h, qi: (bi, 0, n_heads + h)),
        pl.BlockSpec((None, seq, HEAD_DIM), lambda bi, h, qi: (bi, 0, n_heads + n_kv + h)),
    ]
    args = [qkv, qkv, qkv]
    scratch = [pltpu.VMEM((seq // tk, rows, tk), F32)]
    if cache is not None:
        ck, cv = cache
        past = ck.shape[1]
        spec = pl.BlockSpec((None, past, HEAD_DIM), lambda bi, h, qi: (bi, 0, h))
        in_specs += [spec, spec]
        args += [ck.reshape(b, past, n_kv * HEAD_DIM), cv.reshape(b, past, n_kv * HEAD_DIM)]
        scratch.append(pltpu.VMEM((rows, past), F32))
    if sink is not None:
        in_specs.append(pl.BlockSpec((1, groups, LANES), lambda bi, h, qi: (h, 0, 0)))
        args.append(_sink_arg(sink, n_kv, groups))
    scratch += [pltpu.VMEM((rows, LANES), F32), pltpu.VMEM((rows, LANES), F32),
                pltpu.VMEM((rows, HEAD_DIM), F32)]
    return pl.pallas_call(
        functools.partial(_attn_kernel, groups=groups, tq=tq, tk=tk, seq=seq,
                          has_cache=cache is not None, has_sink=sink is not None),
        grid=(b, n_kv, seq // tq),
        in_specs=in_specs,
        out_specs=pl.BlockSpec((None, tq, gw), lambda bi, h, qi: (bi, qi, h)),
        out_shape=jax.ShapeDtypeStruct((b, seq, n_heads * HEAD_DIM), BF16),
        scratch_shapes=scratch,
        compiler_params=_params(3),
        name="attention",
    )(*args)


WIN_QB = 128


def _window_masks(seq):
    band = WIN_QB + 2 * WINDOW
    out = []
    for q0 in (0, WIN_QB, seq - WIN_QB):
        start = int(np.clip(q0 - WINDOW, 0, seq - band))
        qpos = q0 + np.arange(WIN_QB)[:, None]
        kpos = start + np.arange(band)[None, :]
        out.append(np.where(np.abs(qpos - kpos) <= WINDOW, 0.0, NEG_INF))
    return jnp.asarray(np.stack(out), F32)


def _win_kernel(q_ref, k_ref, v_ref, ck_ref, cv_ref, sink_ref, wmask_ref, o_ref, *, groups, seq):
    tq = WIN_QB
    band = tq + 2 * WINDOW
    nb = seq // tq
    past = ck_ref.shape[0]
    kctx = ck_ref[...].astype(BF16)
    vctx = cv_ref[...].astype(BF16)
    sink2 = _sink_rows(sink_ref, groups, tq)

    def body(j, carry):
        q0 = pl.multiple_of(j * tq, tq)
        qs = _stack_heads(q_ref[pl.ds(q0, tq), :], range(groups))
        start = pl.multiple_of(jnp.clip(q0 - WINDOW, 0, seq - band), WINDOW)
        case = jnp.where(j == 0, 0, jnp.where(j == nb - 1, 2, 1))
        s1 = _dot_nt(qs, k_ref[pl.ds(start, band), :]) * QK_SCALE2 + _tile_rows(wmask_ref[case], groups)
        s2 = _dot_nt(qs, kctx) * QK_SCALE2
        mp = functools.reduce(jnp.maximum, _lane_slabs(s1) + _lane_slabs(s2) + [sink2])
        m = jnp.broadcast_to(jnp.max(mp, axis=-1, keepdims=True), mp.shape)
        p1 = jnp.exp2(s1 - _tile_lanes(m, band))
        p2 = jnp.exp2(s2 - _tile_lanes(m, past))
        lp = functools.reduce(jnp.add, _lane_slabs(p1) + _lane_slabs(p2))
        l = jnp.sum(lp, axis=-1, keepdims=True) + jnp.exp2(sink2 - m)[:, :1]
        o = (_dot(p1.astype(BF16), v_ref[pl.ds(start, band), :]) + _dot(p2.astype(BF16), vctx)) / l
        for g in range(groups):
            o_ref[pl.ds(q0, tq), g * HEAD_DIM:(g + 1) * HEAD_DIM] = o[g * tq:(g + 1) * tq].astype(o_ref.dtype)
        return carry
    lax.fori_loop(0, nb, body, 0, unroll=2)


def window_attention(qkv, n_heads, n_kv, cache, sink):
    b, seq, _ = qkv.shape
    groups = n_heads // n_kv
    gw = groups * HEAD_DIM
    band = WIN_QB + 2 * WINDOW
    assert seq % (2 * WIN_QB) == 0 and seq >= band + WIN_QB and qkv.dtype == BF16
    ck, cv = cache
    past = ck.shape[1]
    cspec = pl.BlockSpec((None, past, HEAD_DIM), lambda bi, h: (bi, 0, h))
    return pl.pallas_call(
        functools.partial(_win_kernel, groups=groups, seq=seq),
        grid=(b, n_kv),
        in_specs=[
            pl.BlockSpec((None, seq, gw), lambda bi, h: (bi, 0, h)),
            pl.BlockSpec((None, seq, HEAD_DIM), lambda bi, h: (bi, 0, n_heads + h)),
            pl.BlockSpec((None, seq, HEAD_DIM), lambda bi, h: (bi, 0, n_heads + n_kv + h)),
            cspec, cspec,
            pl.BlockSpec((1, groups, LANES), lambda bi, h: (h, 0, 0)),
            pl.BlockSpec((3, WIN_QB, band), lambda bi, h: (0, 0, 0)),
        ],
        out_specs=pl.BlockSpec((None, seq, gw), lambda bi, h: (bi, 0, h)),
        out_shape=jax.ShapeDtypeStruct((b, seq, n_heads * HEAD_DIM), BF16),
        compiler_params=_params(2),
        name="window_attention",
    )(qkv, qkv, qkv, ck.reshape(b, past, n_kv * HEAD_DIM), cv.reshape(b, past, n_kv * HEAD_DIM),
      _sink_arg(sink, n_kv, groups), _window_masks(seq))


def _diff_kernel(*refs, groups, tq, tk, seq, has_cache, lam_init):
    refs = list(refs)
    q_ref, k_ref, v_ref = refs[:3]
    pos = 3
    if has_cache:
        ck_ref, cv_ref = refs[pos:pos + 2]
        pos += 2
    lq1_ref, lk1_ref, lq2_ref, lk2_ref, sub_ref = refs[pos:pos + 5]
    pos += 5
    o_ref, s_ref = refs[pos:pos + 2]
    pos += 2
    if has_cache:
        sc_ref = refs[pos]
        pos += 1
    m_ref, l_ref, acc_ref = refs[pos:]
    rows = groups * tq
    dv = 2 * HEAD_DIM
    n_chunks = seq // tk

    q = q_ref[...].astype(BF16)
    outs = []
    for i in range(2):
        cols = slice(i * HEAD_DIM, (i + 1) * HEAD_DIM)
        qs = _stack_heads(q, [2 * g + i for g in range(groups)])
        m_ref[...] = jnp.full((rows, LANES), NEG_INF, F32)
        l_ref[...] = jnp.zeros((rows, LANES), F32)
        acc_ref[...] = jnp.zeros((rows, dv), F32)

        def p1(c, carry, qs=qs, cols=cols):
            off = pl.multiple_of(c * tk, tk)
            _logits_pass(qs, k_ref[pl.ds(off, tk), cols].astype(BF16), s_ref.at[c], m_ref)
            return carry
        lax.fori_loop(0, n_chunks, p1, 0)
        if has_cache:
            _logits_pass(qs, ck_ref[:, cols].astype(BF16), sc_ref, m_ref)
        _finish_max(m_ref)

        def p2(c, carry):
            off = pl.multiple_of(c * tk, tk)
            _value_pass(s_ref.at[c], v_ref[pl.ds(off, tk), :].astype(BF16), m_ref, l_ref, acc_ref)
            return carry
        lax.fori_loop(0, n_chunks, p2, 0)
        if has_cache:
            _value_pass(sc_ref, cv_ref[...].astype(BF16), m_ref, l_ref, acc_ref)
        outs.append(acc_ref[...] / _row_sum(l_ref))

    e1 = jnp.exp(jnp.sum(lq1_ref[...] * lk1_ref[...], axis=-1, keepdims=True))
    e2 = jnp.exp(jnp.sum(lq2_ref[...] * lk2_ref[...], axis=-1, keepdims=True))
    lam = e1 - e2 + lam_init
    o = outs[0] - lam * outs[1]
    o = ((o * _rms(o, dv)) * sub_ref[...]) * (1.0 - lam_init)
    for g in range(groups):
        o_ref[:, g * dv:(g + 1) * dv] = o[g * tq:(g + 1) * tq].astype(o_ref.dtype)


def diff_attention(qkv, n_heads, n_kv, lam_vecs, subln, lam_init, cache=None):
    b, seq, _ = qkv.shape
    groups = n_heads // n_kv
    dv = 2 * HEAD_DIM
    tq = _tile(seq, ATTN_TQ)
    tk = _tile(seq, ATTN_TK)
    gw = groups * dv
    rows = groups * tq
    in_specs = [
        pl.BlockSpec((None, tq, gw), lambda bi, h, qi: (bi, qi, h)),
        pl.BlockSpec((None, seq, dv), lambda bi, h, qi: (bi, 0, n_heads + h)),
        pl.BlockSpec((None, seq, dv), lambda bi, h, qi: (bi, 0, n_heads + n_kv + h)),
    ]
    args = [qkv, qkv, qkv]
    scratch = [pltpu.VMEM((seq // tk, rows, tk), F32)]
    if cache is not None:
        ck, cv = cache
        past = ck.shape[1]
        spec = pl.BlockSpec((None, past, dv), lambda bi, h, qi: (bi, 0, h))
        in_specs += [spec, spec]
        args += [ck.reshape(b, past, n_kv * dv), cv.reshape(b, past, n_kv * dv)]
        scratch.append(pltpu.VMEM((rows, past), F32))
    vec = pl.BlockSpec((1, HEAD_DIM), lambda bi, h, qi: (0, 0))
    in_specs += [vec] * 4 + [pl.BlockSpec((1, dv), lambda bi, h, qi: (0, 0))]
    args += [v.astype(F32).reshape(1, HEAD_DIM) for v in lam_vecs] + [subln.astype(F32).reshape(1, dv)]
    scratch += [pltpu.VMEM((rows, LANES), F32), pltpu.VMEM((rows, LANES), F32),
                pltpu.VMEM((rows, dv), F32)]
    return pl.pallas_call(
        functools.partial(_diff_kernel, groups=groups, tq=tq, tk=tk, seq=seq,
                          has_cache=cache is not None, lam_init=lam_init),
        grid=(b, n_kv, seq // tq),
        in_specs=in_specs,
        out_specs=pl.BlockSpec((None, tq, gw), lambda bi, h, qi: (bi, qi, h)),
        out_shape=jax.ShapeDtypeStruct((b, seq, n_heads * dv), BF16),
        scratch_shapes=scratch,
        compiler_params=_params(3),
        name="diff_attention",
    )(*args)


NAT_QB = 256


def _nat_geometry(seq):
    rows = seq // GRID_W
    kr_n = min(NA_ROWS, rows)
    q_rows = NAT_QB // GRID_W
    span = min(kr_n + q_rows - 1, rows)
    return rows, kr_n, q_rows, span


def _nat_band_start(j, seq):
    rows, kr_n, q_rows, span = _nat_geometry(seq)
    clip = jnp.clip if isinstance(j, jax.Array) else np.clip
    mini = jnp.minimum if isinstance(j, jax.Array) else np.minimum
    return mini(clip(j * q_rows - kr_n // 2, 0, rows - kr_n), rows - span)


def _nat_block_pattern(j, seq):
    rows, kr_n, q_rows, span = _nat_geometry(seq)
    qr = j * q_rows + np.arange(q_rows)
    rs = np.clip(qr - kr_n // 2, 0, rows - kr_n)
    kr = int(_nat_band_start(j, seq)) + np.arange(span)
    ok_r = (kr[None, :] >= rs[:, None]) & (kr[None, :] < rs[:, None] + kr_n)
    dr = np.clip(kr[None, :] - qr[:, None] + NA_ROWS - 1, 0, 2 * NA_ROWS - 2)
    return ok_r, dr


def _nat_classes(seq):
    nb = seq // NAT_QB
    reps, change = [], []
    for j in range(nb):
        ok_r, dr = _nat_block_pattern(j, seq)
        key = (ok_r.tobytes(), np.where(ok_r, dr, 0).tobytes())
        if not reps or key != reps[-1][0]:
            assert all(key != k for k, _ in reps), "window patterns must be contiguous runs of blocks"
            reps.append((key, j))
            change.append(j)
    return [j for _, j in reps], change[1:]


def nat_bias_tables(rpb, seq):
    _, _, q_rows, span = _nat_geometry(seq)
    n_heads = rpb.shape[0]
    cl = np.arange(GRID_W)
    cs = np.clip(cl - NA_COLS // 2, 0, GRID_W - NA_COLS)
    ok_c = (cl[None, :] >= cs[:, None]) & (cl[None, :] < cs[:, None] + NA_COLS)
    dc = np.clip(cl[None, :] - cl[:, None] + NA_COLS - 1, 0, 2 * NA_COLS - 2)
    by_col = rpb.astype(F32)[:, :, dc] * LOG2E
    reps, _ = _nat_classes(seq)
    tabs = []
    for j in reps:
        ok_r, dr = _nat_block_pattern(j, seq)
        t = by_col[:, dr]
        ok = ok_r[:, :, None, None] & ok_c[None, None]
        t = jnp.where(ok[None], t, NEG_INF)
        tabs.append(t.transpose(0, 1, 3, 2, 4).reshape(n_heads, q_rows * GRID_W, span * GRID_W))
    return jnp.stack(tabs, axis=0)


def _nat_kernel(q_ref, k_ref, v_ref, ck_ref, cv_ref, t_ref, o_ref, *, seq, change):
    nb = seq // NAT_QB
    band = t_ref.shape[-1]
    kctx = ck_ref[...].astype(BF16)
    vctx = cv_ref[...].astype(BF16)

    def body(j, carry):
        q0 = pl.multiple_of(j * NAT_QB, NAT_QB)
        q = q_ref[pl.ds(q0, NAT_QB), :]
        k0 = pl.multiple_of(_nat_band_start(j, seq) * GRID_W, GRID_W)
        cls = functools.reduce(lambda a, cp: a + jnp.where(j >= cp, 1, 0), change, 0)
        s1 = _dot_nt(q, k_ref[pl.ds(k0, band), :]) * QK_SCALE2 + t_ref[cls, 0]
        s2 = _dot_nt(q, kctx) * QK_SCALE2
        m = jnp.maximum(jnp.max(s1, axis=-1, keepdims=True), jnp.max(s2, axis=-1, keepdims=True))
        p1 = jnp.exp2(s1 - m)
        p2 = jnp.exp2(s2 - m)
        l = jnp.sum(p1, axis=-1, keepdims=True) + jnp.sum(p2, axis=-1, keepdims=True)
        o = (_dot(p1.astype(BF16), v_ref[pl.ds(k0, band), :]) + _dot(p2.astype(BF16), vctx)) / l
        o_ref[pl.ds(q0, NAT_QB), :] = o.astype(o_ref.dtype)
        return carry
    lax.fori_loop(0, nb, body, 0, unroll=2)


def nat_attention(qkv, n_heads, cache, tables):
    b, seq, _ = qkv.shape
    ck, cv = cache
    past = ck.shape[1]
    ncls, _, _, band = tables.shape
    _, change = _nat_classes(seq)
    blk = lambda off: pl.BlockSpec((None, seq, HEAD_DIM), lambda bi, h: (bi, 0, off + h))
    cspec = pl.BlockSpec((None, past, HEAD_DIM), lambda bi, h: (bi, 0, h))
    return pl.pallas_call(
        functools.partial(_nat_kernel, seq=seq, change=tuple(change)),
        grid=(b, n_heads),
        in_specs=[blk(0), blk(n_heads), blk(2 * n_heads), cspec, cspec,
                  pl.BlockSpec((ncls, 1, NAT_QB, band), lambda bi, h: (0, h, 0, 0))],
        out_specs=pl.BlockSpec((None, seq, HEAD_DIM), lambda bi, h: (bi, 0, h)),
        out_shape=jax.ShapeDtypeStruct((b, seq, n_heads * HEAD_DIM), BF16),
        compiler_params=_params(2),
        name="nat_attention",
    )(qkv, qkv, qkv, ck.reshape(b, past, n_heads * HEAD_DIM), cv.reshape(b, past, n_heads * HEAD_DIM), tables)


def kernel(x_prompt, x_sample, cache_k0, cache_v0, cache_k1, cache_v1, cache_k2, cache_v2, cache_k3, cache_v3, c, c_ctx, norm_g, w_ada, b_ada, w_ffn_in, w_ffn_out, att_wqkv, att_wo, att_qn, att_kn, win_wqkv, win_wo, win_qn, win_kn, win_sink, diff_wqkv, diff_wo, diff_qn, diff_kn, diff_lq1, diff_lk1, diff_lq2, diff_lk2, diff_subln, nat_wqkv, nat_wo, nat_qn, nat_kn, nat_rpb):
    depth = w_ada.shape[0]
    d = x_prompt.shape[-1]
    bp, sp, _ = x_prompt.shape
    bs, ss, _ = x_sample.shape
    caches = [(cache_k0, cache_v0), (cache_k1, cache_v1), (cache_k2, cache_v2), (cache_k3, cache_v3)]
    kv_heads = [cache_k0.shape[2], cache_k1.shape[2], cache_k2.shape[2], cache_k3.shape[2]]
    wqkv = [w.astype(BF16) for w in (att_wqkv, win_wqkv, diff_wqkv, nat_wqkv)]
    wo = [w.astype(BF16) for w in (att_wo, win_wo, diff_wo, nat_wo)]
    q_cols = [w.shape[0] for w in wo]
    qn = [att_qn, win_qn, diff_qn, nat_qn]
    kn = [att_kn, win_kn, diff_kn, nat_kn]
    w_in = w_ffn_in.astype(BF16)
    w_out = w_ffn_out.astype(BF16)
    lam_vecs = (diff_lq1, diff_lk1, diff_lq2, diff_lk2)

    rows = 8 * (-(-(1 + bs) // 8))
    cond = jnp.zeros((rows, d), F32).at[0].set(c_ctx).at[1:1 + bs].set(c)
    mods = adaln(cond, w_ada, b_ada).reshape(depth, rows, N_MOD, d)

    rope = rope_tables(ss)
    tables = nat_bias_tables(nat_rpb, ss)

    def run(x3, latent):
        b, seq, _ = x3.shape
        mseq = seq if latent else b * seq

        def mod(l, chunk):
            r = mods[l, 1:1 + bs, chunk] if latent else mods[l, 0:1, chunk]
            return r[:, None, :]

        x = x3.reshape(b * seq, d)
        h = normmod(x, norm_g[0, 0], mod(0, 0), mod(0, 1), mseq)
        state = []
        for l in range(depth):
            kind = l % 4
            n_kv = kv_heads[kind]
            a = ffn_in(h, w_in[l, 0])
            x, h = resid_norm(a, w_out[l, 0], x, mod(l, 2), 0.5, mseq,
                              norm=(norm_g[l, 1], mod(l, 3), mod(l, 4)))
            kv_cols = (wqkv[kind].shape[1] - q_cols[kind]) // 2
            use_rope = latent and kind != 3
            qkv = qkv_proj(h, wqkv[kind], qn[kind], kn[kind], q_cols[kind], kv_cols,
                           seq if use_rope else mseq, rope if use_rope else None,
                           BF16 if latent else F32)
            qkv = qkv.reshape(b, seq, -1)
            cache = caches[kind] if latent else None
            if kind == 2:
                lam_init = 0.8 - 0.6 * math.exp(-0.3 * l)
                n_heads = q_cols[kind] // (2 * HEAD_DIM)
                o = diff_attention(qkv, n_heads, n_kv, lam_vecs, diff_subln, lam_init, cache)
            elif kind == 3 and latent:
                n_heads = q_cols[kind] // HEAD_DIM
                o = nat_attention(qkv, n_heads, cache, tables)
            elif kind == 1 and latent:
                n_heads = q_cols[kind] // HEAD_DIM
                o = window_attention(qkv, n_heads, n_kv, cache, win_sink)
            else:
                n_heads = q_cols[kind] // HEAD_DIM
                o = attention(qkv, n_heads, n_kv, cache, sink=win_sink if kind == 1 else None)
            if not latent:
                nq = q_cols[kind]
                state.append(qkv[:, :, nq:nq + kv_cols])
                state.append(qkv[:, :, nq + kv_cols:])
            x, h = resid_norm(o.reshape(b * seq, -1), wo[kind], x, mod(l, 5), 1.0, mseq,
                              norm=(norm_g[l, 2], mod(l, 6), mod(l, 7)))
            a = ffn_in(h, w_in[l, 1])
            if l + 1 < depth:
                x, h = resid_norm(a, w_out[l, 1], x, mod(l, 8), 0.5, mseq,
                                  norm=(norm_g[l + 1, 0], mod(l + 1, 0), mod(l + 1, 1)))
            else:
                x = resid_norm(a, w_out[l, 1], x, mod(l, 8), 0.5, mseq)
        return x.reshape(b, seq, d), state

    yp, st = run(x_prompt, False)
    ys, _ = run(x_sample, True)
    hd = HEAD_DIM
    new_state = [
        st[0].reshape(bp, sp, kv_heads[0], hd), st[1].reshape(bp, sp, kv_heads[0], hd),
        st[2].reshape(bp, sp, kv_heads[1], hd), st[3].reshape(bp, sp, kv_heads[1], hd),
        st[4].reshape(bp, sp, kv_heads[2], 2, hd), st[5].reshape(bp, sp, kv_heads[2], 2 * hd),
        st[6].reshape(bp, sp, kv_heads[3], hd), st[7].reshape(bp, sp, kv_heads[3], hd),
    ]
    return (yp, ys, *new_state)
```

```python
import functools
import math

import jax
import jax.numpy as jnp
import numpy as np
from jax import lax
from jax.experimental import pallas as pl
from jax.experimental.pallas import tpu as pltpu

F32 = jnp.float32
BF16 = jnp.bfloat16

HEAD_DIM = 128
LANES = 128
GRID_W = 64
WINDOW = 128
NA_ROWS = 8
NA_COLS = 16
ROPE_THETA = 10000.0
EPS = 1e-6
N_MOD = 9
NEG_INF = -1e30
LOG2E = math.log2(math.e)
QK_SCALE2 = HEAD_DIM ** -0.5 * LOG2E

V7X_VMEM_BYTES = 64 * 1024 * 1024
VMEM_LIMIT = V7X_VMEM_BYTES - 8 * 1024 * 1024


def _params(n_grid):
    return pltpu.CompilerParams(
        dimension_semantics=("arbitrary",) * n_grid, vmem_limit_bytes=VMEM_LIMIT)


def _tile(n, pref):
    t = min(n, pref)
    while n % t:
        t //= 2
    return t


def _dot(a, b):
    return jnp.dot(a, b, preferred_element_type=F32)


def _dot_nt(a, b):
    return lax.dot_general(a, b, (((1,), (1,)), ((), ())), preferred_element_type=F32)


def _silu(x):
    return x * jax.nn.sigmoid(x)


def _rms(x, n):
    return lax.rsqrt(jnp.sum(x * x, axis=-1, keepdims=True) * (1.0 / n) + EPS)


def _adaln_kernel(c_ref, w_ref, b_ref, o_ref):
    a = _silu(c_ref[...]).astype(BF16)
    o_ref[0] = _dot(a, w_ref[0].astype(BF16)) + b_ref[0]


def adaln(cond, w_ada, b_ada):
    depth, d, n = w_ada.shape
    r = cond.shape[0]
    tn = _tile(n, 512)
    return pl.pallas_call(
        _adaln_kernel,
        grid=(depth, n // tn),
        in_specs=[
            pl.BlockSpec((r, d), lambda l, j: (0, 0)),
            pl.BlockSpec((1, d, tn), lambda l, j: (l, 0, j)),
            pl.BlockSpec((1, 1, tn), lambda l, j: (l, 0, j)),
        ],
        out_specs=pl.BlockSpec((1, r, tn), lambda l, j: (l, 0, j)),
        out_shape=jax.ShapeDtypeStruct((depth, r, n), F32),
        compiler_params=_params(2),
        name="adaln",
    )(cond, w_ada, b_ada.reshape(depth, 1, n))


def _modulated_norm(x, g, shift, scale):
    return (x * _rms(x, x.shape[-1])) * (g * (1.0 + scale)) + shift


def _normmod_kernel(x_ref, g_ref, sh_ref, sc_ref, h_ref):
    h_ref[...] = _modulated_norm(x_ref[...], g_ref[...], sh_ref[0], sc_ref[0]).astype(BF16)


def normmod(x, g, shift, scale, seq):
    t, d = x.shape
    tm = _tile(seq, 512)
    per = seq // tm
    vec = pl.BlockSpec((1, 1, d), lambda i: (i // per, 0, 0))
    return pl.pallas_call(
        _normmod_kernel,
        grid=(t // tm,),
        in_specs=[pl.BlockSpec((tm, d), lambda i: (i, 0)),
                  pl.BlockSpec((1, d), lambda i: (0, 0)), vec, vec],
        out_specs=pl.BlockSpec((tm, d), lambda i: (i, 0)),
        out_shape=jax.ShapeDtypeStruct((t, d), BF16),
        compiler_params=_params(1),
        name="normmod",
    )(x, g.reshape(1, d), shift, scale)


def _ffn_in_kernel(h_ref, wg_ref, wu_ref, a_ref):
    h = h_ref[...]
    gate = _dot(h, wg_ref[...])
    up = _dot(h, wu_ref[...])
    a_ref[...] = (_silu(gate) * up).astype(BF16)


def ffn_in(h, w_in, idx):
    t, d = h.shape
    f = w_in.shape[-1] // 2
    tm = _tile(t, 1024)
    tn = _tile(f, 512)
    nj = f // tn
    l, m = idx
    return pl.pallas_call(
        _ffn_in_kernel,
        grid=(t // tm, nj),
        in_specs=[
            pl.BlockSpec((tm, d), lambda i, j: (i, 0)),
            pl.BlockSpec((None, None, d, tn), lambda i, j: (l, m, 0, j)),
            pl.BlockSpec((None, None, d, tn), lambda i, j: (l, m, 0, j + nj)),
        ],
        out_specs=pl.BlockSpec((tm, tn), lambda i, j: (i, j)),
        out_shape=jax.ShapeDtypeStruct((t, f), BF16),
        compiler_params=_params(2),
        name="ffn_in",
    )(h, w_in, w_in)


RESID_ROW_SPLIT = 2


def _by_parity(step, fn, bufs):
    parity = lax.rem(step, 2)

    @pl.when(parity == 0)
    def _():
        fn(bufs[0], bufs[1])

    @pl.when(parity == 1)
    def _():
        fn(bufs[1], bufs[0])


def _resid_kernel(a_ref, w_ref, x_ref, gate_ref, *rest, coef, with_norm):
    tm = a_ref.shape[0]
    split = RESID_ROW_SPLIT if tm % (16 * RESID_ROW_SPLIT) == 0 else 1
    sub = tm // split
    gate = coef * gate_ref[0]
    for r in range(split):
        rs = slice(r * sub, (r + 1) * sub)
        y = _dot(a_ref[rs, :].astype(BF16), w_ref[...])
        xo = x_ref[rs, :] + gate * y
        if with_norm:
            g_ref, sh_ref, sc_ref, xo_ref, h_ref = rest
            xo_ref[rs, :] = xo
            h_ref[rs, :] = _modulated_norm(xo, g_ref[...], sh_ref[0], sc_ref[0]).astype(BF16)
        else:
            (xo_ref,) = rest
            xo_ref[rs, :] = xo


def resid_norm(a, w, idx, x, gate, coef, seq, norm=None):
    t, k = a.shape
    d = w.shape[-1]
    l, m = idx
    tm = _tile(seq, 256)
    per = seq // tm
    vec = pl.BlockSpec((1, 1, d), lambda i: (i // per, 0, 0))
    row = pl.BlockSpec((tm, d), lambda i: (i, 0))
    in_specs = [
        pl.BlockSpec((tm, k), lambda i: (i, 0)),
        pl.BlockSpec((None, None, k, d), lambda i: (l, m, 0, 0), pipeline_mode=pl.Buffered(1)),
        row, vec,
    ]
    args = [a, w, x, gate]
    if norm is not None:
        g, shift, scale = norm
        in_specs += [pl.BlockSpec((1, d), lambda i: (0, 0)), vec, vec]
        args += [g.reshape(1, d), shift, scale]
        out_specs = [row, row]
        out_shape = [jax.ShapeDtypeStruct((t, d), F32), jax.ShapeDtypeStruct((t, d), BF16)]
    else:
        out_specs = row
        out_shape = jax.ShapeDtypeStruct((t, d), F32)
    return pl.pallas_call(
        functools.partial(_resid_kernel, coef=coef, with_norm=norm is not None),
        grid=(t // tm,),
        in_specs=in_specs,
        out_specs=out_specs,
        out_shape=out_shape,
        compiler_params=_params(1),
        name="resid_norm",
    )(*args)


def _qkv_kernel(h_ref, w_ref, gain_ref, *rest, n_norm, n_col, rope, heads):
    acc_refs = rest[-2:]
    rest = rest[:-2]
    step = pl.program_id(0)
    is_value = lax.rem(jnp.maximum(step - 1, 0), n_col) >= n_norm
    if rope:
        cos_ref, sin_ref, o_ref = rest
    else:
        (o_ref,) = rest

    @pl.when(step == 0)
    def _():
        acc_refs[1][...] = jnp.zeros(acc_refs[1].shape, F32)

    def body(acc_w, acc_r):
        for hh in range(heads):
            sl = slice(hh * HEAD_DIM, (hh + 1) * HEAD_DIM)
            xh = acc_r[:, sl]
            y = (xh * jnp.where(is_value, 1.0, _rms(xh, HEAD_DIM))) * gain_ref[:, sl]
            if rope:
                y = y * cos_ref[...] + pltpu.roll(y, HEAD_DIM // 2, 1) * sin_ref[...]
            o_ref[:, sl] = y.astype(o_ref.dtype)
        acc_w[...] = _dot(h_ref[...], w_ref[...])

    _by_parity(step, body, acc_refs)


def qkv_proj(h, w, qn, kn, n_q, n_k, seq, rope, out_dtype):
    t, d = h.shape
    n = w.shape[1]
    tm = _tile(seq, 1024)
    tn = _tile(math.gcd(n_q, n_k), 512)
    per = seq // tm
    n_norm = (n_q + n_k) // tn
    n_col = n // tn
    n_tiles = (t // tm) * n_col
    gains = jnp.concatenate([jnp.tile(qn.astype(F32), n_q // HEAD_DIM), jnp.tile(kn.astype(F32), n_k // HEAD_DIM),
                             jnp.ones((n - n_q - n_k,), F32)]).reshape(1, n)
    cur = lambda s: jnp.minimum(s, n_tiles - 1)
    lag = lambda s: jnp.maximum(s - 1, 0)
    in_specs = [
        pl.BlockSpec((tm, d), lambda s: (cur(s) // n_col, 0)),
        pl.BlockSpec((d, tn), lambda s: (0, cur(s) % n_col)),
        pl.BlockSpec((1, tn), lambda s: (0, lag(s) % n_col)),
    ]
    args = [h, w, gains]
    if rope is not None:
        cos, sin = rope
        tab = pl.BlockSpec((None, tm, HEAD_DIM),
                           lambda s: (jnp.where(lag(s) % n_col >= n_norm, 1, 0), (lag(s) // n_col) % per, 0))
        in_specs += [tab, tab]
        args += [jnp.stack([cos, jnp.ones_like(cos)]), jnp.stack([sin, jnp.zeros_like(sin)])]
    return pl.pallas_call(
        functools.partial(_qkv_kernel, n_norm=n_norm, n_col=n_col, rope=rope is not None,
                          heads=tn // HEAD_DIM),
        grid=(n_tiles + 1,),
        in_specs=in_specs,
        out_specs=pl.BlockSpec((tm, tn), lambda s: (lag(s) // n_col, lag(s) % n_col)),
        out_shape=jax.ShapeDtypeStruct((t, n), out_dtype),
        scratch_shapes=[pltpu.VMEM((tm, tn), F32), pltpu.VMEM((tm, tn), F32)],
        compiler_params=_params(1),
        name="qkv_proj",
    )(*args)


def rope_tables(seq):
    t = jnp.arange(seq)
    row = (t // GRID_W).astype(F32)
    col = (t % GRID_W).astype(F32)
    nf = HEAD_DIM // 4
    inv = ROPE_THETA ** (-jnp.arange(nf, dtype=F32) / nf)
    ang = jnp.concatenate([row[:, None] * inv, col[:, None] * inv], axis=-1)
    cos, sin = jnp.cos(ang), jnp.sin(ang)
    return jnp.concatenate([cos, cos], axis=-1), jnp.concatenate([-sin, sin], axis=-1)


ATTN_TQ = 256
ATTN_TK = 1024
PASS_UNROLL = 2


def _stack_heads(q, idx):
    parts = [q[:, i * HEAD_DIM:(i + 1) * HEAD_DIM] for i in idx]
    return parts[0] if len(parts) == 1 else jnp.concatenate(parts, axis=0)


def _lane_slabs(x):
    return [x[:, i * LANES:(i + 1) * LANES] for i in range(x.shape[1] // LANES)]


def _tile_lanes(x, width):
    n = width // LANES
    return x if n == 1 else jnp.concatenate([x] * n, axis=1)


def _tile_rows(x, n):
    return x if n == 1 else jnp.concatenate([x] * n, axis=0)


def _logits_pass(qs, kc, s_out, m_ref, bias=None):
    s = _dot_nt(qs, kc) * QK_SCALE2
    if bias is not None:
        s = s + bias
    s_out[...] = s
    m_ref[...] = jnp.maximum(m_ref[...], functools.reduce(jnp.maximum, _lane_slabs(s)))


def _finish_max(m_ref):
    m_ref[...] = jnp.broadcast_to(jnp.max(m_ref[...], axis=-1, keepdims=True), m_ref.shape)


def _value_pass(s_in, vc, m_ref, l_ref, acc_ref):
    s = s_in[...]
    p = jnp.exp2(s - _tile_lanes(m_ref[...], s.shape[1]))
    l_ref[...] += functools.reduce(jnp.add, _lane_slabs(p))
    acc_ref[...] += _dot(p.astype(BF16), vc)


def _row_sum(l_ref):
    return jnp.sum(l_ref[...], axis=-1, keepdims=True)


def _sink_rows(sink_ref, groups, tq):
    sk = sink_ref[0]
    parts = [jnp.broadcast_to(sk[g:g + 1, :], (tq, LANES)) for g in range(groups)]
    return parts[0] if groups == 1 else jnp.concatenate(parts, axis=0)


def _attn_kernel(*refs, groups, tq, tk, seq, has_cache, has_sink):
    refs = list(refs)
    q_ref, k_ref, v_ref = refs[:3]
    pos = 3
    if has_cache:
        ck_ref, cv_ref = refs[pos:pos + 2]
        pos += 2
    if has_sink:
        sink_ref = refs[pos]
        pos += 1
    o_ref, s_ref = refs[pos:pos + 2]
    pos += 2
    if has_cache:
        sc_ref = refs[pos]
        pos += 1
    m_ref, l_ref, acc_ref = refs[pos:]
    rows = groups * tq
    n_chunks = seq // tk

    qs = _stack_heads(q_ref[...].astype(BF16), range(groups))
    if has_sink:
        sink2 = _sink_rows(sink_ref, groups, tq)
        m_ref[...] = sink2
    else:
        m_ref[...] = jnp.full((rows, LANES), NEG_INF, F32)
    l_ref[...] = jnp.zeros((rows, LANES), F32)
    acc_ref[...] = jnp.zeros(acc_ref.shape, F32)

    def p1(c, carry):
        off = pl.multiple_of(c * tk, tk)
        _logits_pass(qs, k_ref[pl.ds(off, tk), :].astype(BF16), s_ref.at[c], m_ref)
        return carry
    lax.fori_loop(0, n_chunks, p1, 0, unroll=PASS_UNROLL)
    if has_cache:
        _logits_pass(qs, ck_ref[...].astype(BF16), sc_ref, m_ref)
    _finish_max(m_ref)

    def p2(c, carry):
        off = pl.multiple_of(c * tk, tk)
        _value_pass(s_ref.at[c], v_ref[pl.ds(off, tk), :].astype(BF16), m_ref, l_ref, acc_ref)
        return carry
    lax.fori_loop(0, n_chunks, p2, 0, unroll=PASS_UNROLL)
    if has_cache:
        _value_pass(sc_ref, cv_ref[...].astype(BF16), m_ref, l_ref, acc_ref)

    l = _row_sum(l_ref)
    if has_sink:
        l = l + jnp.exp2(sink2 - m_ref[...])[:, :1]
    o = acc_ref[...] / l
    for g in range(groups):
        o_ref[:, g * HEAD_DIM:(g + 1) * HEAD_DIM] = o[g * tq:(g + 1) * tq].astype(o_ref.dtype)


def _sink_arg(sink, n_kv, groups):
    sink2 = sink.astype(F32).reshape(n_kv, groups, 1) * LOG2E
    return jnp.broadcast_to(sink2, (n_kv, groups, LANES))


def attention(qkv, n_heads, n_kv, cache=None, sink=None):
    b, seq, _ = qkv.shape
    groups = n_heads // n_kv
    tq = _tile(seq, ATTN_TQ)
    tk = _tile(seq, ATTN_TK)
    gw = groups * HEAD_DIM
    rows = groups * tq
    in_specs = [
        pl.BlockSpec((None, tq, gw), lambda bi, h, qi: (bi, qi, h)),
        pl.BlockSpec((None, seq, HEAD_DIM), lambda bi, h, qi: (bi, 0, n_heads + h)),
        pl.BlockSpec((None, seq, HEAD_DIM), lambda bi, h, qi: (bi, 0, n_heads + n_kv + h)),
    ]
    args = [qkv, qkv, qkv]
    scratch = [pltpu.VMEM((seq // tk, rows, tk), F32)]
    if cache is not None:
        ck, cv = cache
        past = ck.shape[1]
        spec = pl.BlockSpec((None, past, HEAD_DIM), lambda bi, h, qi: (bi, 0, h))
        in_specs += [spec, spec]
        args += [ck.reshape(b, past, n_kv * HEAD_DIM), cv.reshape(b, past, n_kv * HEAD_DIM)]
        scratch.append(pltpu.VMEM((rows, past), F32))
    if sink is not None:
        in_specs.append(pl.BlockSpec((1, groups, LANES), lambda bi, h, qi: (h, 0, 0)))
        args.append(_sink_arg(sink, n_kv, groups))
    scratch += [pltpu.VMEM((rows, LANES), F32), pltpu.VMEM((rows, LANES), F32),
                pltpu.VMEM((rows, HEAD_DIM), F32)]
    return pl.pallas_call(
        functools.partial(_attn_kernel, groups=groups, tq=tq, tk=tk, seq=seq,
                          has_cache=cache is not None, has_sink=sink is not None),
        grid=(b, n_kv, seq // tq),
        in_specs=in_specs,
        out_specs=pl.BlockSpec((None, tq, gw), lambda bi, h, qi: (bi, qi, h)),
        out_shape=jax.ShapeDtypeStruct((b, seq, n_heads * HEAD_DIM), BF16),
        scratch_shapes=scratch,
        compiler_params=_params(3),
        name="attention",
    )(*args)


WIN_QB = 128


def _window_masks(seq):
    band = WIN_QB + 2 * WINDOW
    out = []
    for q0 in (0, WIN_QB, seq - WIN_QB):
        start = int(np.clip(q0 - WINDOW, 0, seq - band))
        qpos = q0 + np.arange(WIN_QB)[:, None]
        kpos = start + np.arange(band)[None, :]
        out.append(np.where(np.abs(qpos - kpos) <= WINDOW, 0.0, NEG_INF))
    return jnp.asarray(np.stack(out), F32)


def _win_kernel(q_ref, k_ref, v_ref, ck_ref, cv_ref, sink_ref, wmask_ref, o_ref, *, groups, seq):
    tq = WIN_QB
    band = tq + 2 * WINDOW
    nb = seq // tq
    past = ck_ref.shape[0]
    kctx = ck_ref[...].astype(BF16)
    vctx = cv_ref[...].astype(BF16)
    sink2 = _sink_rows(sink_ref, groups, tq)

    def body(j, carry):
        q0 = pl.multiple_of(j * tq, tq)
        qs = _stack_heads(q_ref[pl.ds(q0, tq), :], range(groups))
        start = pl.multiple_of(jnp.clip(q0 - WINDOW, 0, seq - band), WINDOW)
        case = jnp.where(j == 0, 0, jnp.where(j == nb - 1, 2, 1))
        s1 = _dot_nt(qs, k_ref[pl.ds(start, band), :]) * QK_SCALE2 + _tile_rows(wmask_ref[case], groups)
        s2 = _dot_nt(qs, kctx) * QK_SCALE2
        mp = functools.reduce(jnp.maximum, _lane_slabs(s1) + _lane_slabs(s2) + [sink2])
        m = jnp.broadcast_to(jnp.max(mp, axis=-1, keepdims=True), mp.shape)
        p1 = jnp.exp2(s1 - _tile_lanes(m, band))
        p2 = jnp.exp2(s2 - _tile_lanes(m, past))
        lp = functools.reduce(jnp.add, _lane_slabs(p1) + _lane_slabs(p2))
        l = jnp.sum(lp, axis=-1, keepdims=True) + jnp.exp2(sink2 - m)[:, :1]
        o = (_dot(p1.astype(BF16), v_ref[pl.ds(start, band), :]) + _dot(p2.astype(BF16), vctx)) / l
        for g in range(groups):
            o_ref[pl.ds(q0, tq), g * HEAD_DIM:(g + 1) * HEAD_DIM] = o[g * tq:(g + 1) * tq].astype(o_ref.dtype)
        return carry
    lax.fori_loop(0, nb, body, 0, unroll=4)


def window_attention(qkv, n_heads, n_kv, cache, sink):
    b, seq, _ = qkv.shape
    groups = n_heads // n_kv
    gw = groups * HEAD_DIM
    band = WIN_QB + 2 * WINDOW
    assert seq % (2 * WIN_QB) == 0 and seq >= band + WIN_QB and qkv.dtype == BF16
    ck, cv = cache
    past = ck.shape[1]
    cspec = pl.BlockSpec((None, past, HEAD_DIM), lambda bi, h: (bi, 0, h))
    return pl.pallas_call(
        functools.partial(_win_kernel, groups=groups, seq=seq),
        grid=(b, n_kv),
        in_specs=[
            pl.BlockSpec((None, seq, gw), lambda bi, h: (bi, 0, h)),
            pl.BlockSpec((None, seq, HEAD_DIM), lambda bi, h: (bi, 0, n_heads + h)),
            pl.BlockSpec((None, seq, HEAD_DIM), lambda bi, h: (bi, 0, n_heads + n_kv + h)),
            cspec, cspec,
            pl.BlockSpec((1, groups, LANES), lambda bi, h: (h, 0, 0)),
            pl.BlockSpec((3, WIN_QB, band), lambda bi, h: (0, 0, 0)),
        ],
        out_specs=pl.BlockSpec((None, seq, gw), lambda bi, h: (bi, 0, h)),
        out_shape=jax.ShapeDtypeStruct((b, seq, n_heads * HEAD_DIM), BF16),
        compiler_params=_params(2),
        name="window_attention",
    )(qkv, qkv, qkv, ck.reshape(b, past, n_kv * HEAD_DIM), cv.reshape(b, past, n_kv * HEAD_DIM),
      _sink_arg(sink, n_kv, groups), _window_masks(seq))


def _diff_kernel(*refs, groups, tq, tk, seq, has_cache, lam_init):
    refs = list(refs)
    q_ref, k_ref, v_ref = refs[:3]
    pos = 3
    if has_cache:
        ck_ref, cv_ref = refs[pos:pos + 2]
        pos += 2
    lq1_ref, lk1_ref, lq2_ref, lk2_ref, sub_ref = refs[pos:pos + 5]
    pos += 5
    o_ref, s_ref = refs[pos:pos + 2]
    pos += 2
    if has_cache:
        sc_ref = refs[pos]
        pos += 1
    m_ref, l_ref, acc_ref = refs[pos:]
    rows = groups * tq
    dv = 2 * HEAD_DIM
    n_chunks = seq // tk

    q = q_ref[...].astype(BF16)
    outs = []
    for i in range(2):
        cols = slice(i * HEAD_DIM, (i + 1) * HEAD_DIM)
        qs = _stack_heads(q, [2 * g + i for g in range(groups)])
        m_ref[...] = jnp.full((rows, LANES), NEG_INF, F32)
        l_ref[...] = jnp.zeros((rows, LANES), F32)
        acc_ref[...] = jnp.zeros((rows, dv), F32)

        def p1(c, carry, qs=qs, cols=cols):
            off = pl.multiple_of(c * tk, tk)
            _logits_pass(qs, k_ref[pl.ds(off, tk), cols].astype(BF16), s_ref.at[c], m_ref)
            return carry
        lax.fori_loop(0, n_chunks, p1, 0, unroll=PASS_UNROLL)
        if has_cache:
            _logits_pass(qs, ck_ref[:, cols].astype(BF16), sc_ref, m_ref)
        _finish_max(m_ref)

        def p2(c, carry):
            off = pl.multiple_of(c * tk, tk)
            _value_pass(s_ref.at[c], v_ref[pl.ds(off, tk), :].astype(BF16), m_ref, l_ref, acc_ref)
            return carry
        lax.fori_loop(0, n_chunks, p2, 0, unroll=PASS_UNROLL)
        if has_cache:
            _value_pass(sc_ref, cv_ref[...].astype(BF16), m_ref, l_ref, acc_ref)
        outs.append(acc_ref[...] / _row_sum(l_ref))

    e1 = jnp.exp(jnp.sum(lq1_ref[...] * lk1_ref[...], axis=-1, keepdims=True))
    e2 = jnp.exp(jnp.sum(lq2_ref[...] * lk2_ref[...], axis=-1, keepdims=True))
    lam = e1 - e2 + lam_init
    o = outs[0] - lam * outs[1]
    o = ((o * _rms(o, dv)) * sub_ref[...]) * (1.0 - lam_init)
    for g in range(groups):
        o_ref[:, g * dv:(g + 1) * dv] = o[g * tq:(g + 1) * tq].astype(o_ref.dtype)


def diff_attention(qkv, n_heads, n_kv, lam_vecs, subln, lam_init, cache=None):
    b, seq, _ = qkv.shape
    groups = n_heads // n_kv
    dv = 2 * HEAD_DIM
    tq = _tile(seq, ATTN_TQ)
    tk = _tile(seq, ATTN_TK)
    gw = groups * dv
    rows = groups * tq
    in_specs = [
        pl.BlockSpec((None, tq, gw), lambda bi, h, qi: (bi, qi, h)),
        pl.BlockSpec((None, seq, dv), lambda bi, h, qi: (bi, 0, n_heads + h)),
        pl.BlockSpec((None, seq, dv), lambda bi, h, qi: (bi, 0, n_heads + n_kv + h)),
    ]
    args = [qkv, qkv, qkv]
    scratch = [pltpu.VMEM((seq // tk, rows, tk), F32)]
    if cache is not None:
        ck, cv = cache
        past = ck.shape[1]
        spec = pl.BlockSpec((None, past, dv), lambda bi, h, qi: (bi, 0, h))
        in_specs += [spec, spec]
        args += [ck.reshape(b, past, n_kv * dv), cv.reshape(b, past, n_kv * dv)]
        scratch.append(pltpu.VMEM((rows, past), F32))
    vec = pl.BlockSpec((1, HEAD_DIM), lambda bi, h, qi: (0, 0))
    in_specs += [vec] * 4 + [pl.BlockSpec((1, dv), lambda bi, h, qi: (0, 0))]
    args += [v.astype(F32).reshape(1, HEAD_DIM) for v in lam_vecs] + [subln.astype(F32).reshape(1, dv)]
    scratch += [pltpu.VMEM((rows, LANES), F32), pltpu.VMEM((rows, LANES), F32),
                pltpu.VMEM((rows, dv), F32)]
    return pl.pallas_call(
        functools.partial(_diff_kernel, groups=groups, tq=tq, tk=tk, seq=seq,
                          has_cache=cache is not None, lam_init=lam_init),
        grid=(b, n_kv, seq // tq),
        in_specs=in_specs,
        out_specs=pl.BlockSpec((None, tq, gw), lambda bi, h, qi: (bi, qi, h)),
        out_shape=jax.ShapeDtypeStruct((b, seq, n_heads * dv), BF16),
        scratch_shapes=scratch,
        compiler_params=_params(3),
        name="diff_attention",
    )(*args)


NAT_QB = 256


def _nat_geometry(seq):
    rows = seq // GRID_W
    kr_n = min(NA_ROWS, rows)
    q_rows = NAT_QB // GRID_W
    span = min(kr_n + q_rows - 1, rows)
    return rows, kr_n, q_rows, span


def _nat_band_start(j, seq):
    rows, kr_n, q_rows, span = _nat_geometry(seq)
    clip = jnp.clip if isinstance(j, jax.Array) else np.clip
    mini = jnp.minimum if isinstance(j, jax.Array) else np.minimum
    return mini(clip(j * q_rows - kr_n // 2, 0, rows - kr_n), rows - span)


def _nat_block_pattern(j, seq):
    rows, kr_n, q_rows, span = _nat_geometry(seq)
    qr = j * q_rows + np.arange(q_rows)
    rs = np.clip(qr - kr_n // 2, 0, rows - kr_n)
    kr = int(_nat_band_start(j, seq)) + np.arange(span)
    ok_r = (kr[None, :] >= rs[:, None]) & (kr[None, :] < rs[:, None] + kr_n)
    dr = np.clip(kr[None, :] - qr[:, None] + NA_ROWS - 1, 0, 2 * NA_ROWS - 2)
    return ok_r, dr


def _nat_classes(seq):
    nb = seq // NAT_QB
    reps, change = [], []
    for j in range(nb):
        ok_r, dr = _nat_block_pattern(j, seq)
        key = (ok_r.tobytes(), np.where(ok_r, dr, 0).tobytes())
        if not reps or key != reps[-1][0]:
            assert all(key != k for k, _ in reps), "window patterns must be contiguous runs of blocks"
            reps.append((key, j))
            change.append(j)
    return [j for _, j in reps], change[1:]


def nat_bias_tables(rpb, seq):
    _, _, q_rows, span = _nat_geometry(seq)
    n_heads = rpb.shape[0]
    cl = np.arange(GRID_W)
    cs = np.clip(cl - NA_COLS // 2, 0, GRID_W - NA_COLS)
    ok_c = (cl[None, :] >= cs[:, None]) & (cl[None, :] < cs[:, None] + NA_COLS)
    dc = np.clip(cl[None, :] - cl[:, None] + NA_COLS - 1, 0, 2 * NA_COLS - 2)
    by_col = rpb.astype(F32)[:, :, dc] * LOG2E
    reps, _ = _nat_classes(seq)
    tabs = []
    for j in reps:
        ok_r, dr = _nat_block_pattern(j, seq)
        t = by_col[:, dr]
        ok = ok_r[:, :, None, None] & ok_c[None, None]
        t = jnp.where(ok[None], t, NEG_INF)
        tabs.append(t.transpose(0, 1, 3, 2, 4).reshape(n_heads, q_rows * GRID_W, span * GRID_W))
    return jnp.stack(tabs, axis=0)


def _nat_kernel(q_ref, k_ref, v_ref, ck_ref, cv_ref, t_ref, o_ref, *, seq, change):
    nb = seq // NAT_QB
    band = t_ref.shape[-1]
    kctx = ck_ref[...].astype(BF16)
    vctx = cv_ref[...].astype(BF16)

    def body(j, carry):
        q0 = pl.multiple_of(j * NAT_QB, NAT_QB)
        q = q_ref[pl.ds(q0, NAT_QB), :]
        k0 = pl.multiple_of(_nat_band_start(j, seq) * GRID_W, GRID_W)
        cls = functools.reduce(lambda a, cp: a + jnp.where(j >= cp, 1, 0), change, 0)
        s1 = _dot_nt(q, k_ref[pl.ds(k0, band), :]) * QK_SCALE2 + t_ref[cls, 0]
        s2 = _dot_nt(q, kctx) * QK_SCALE2
        m = jnp.maximum(jnp.max(s1, axis=-1, keepdims=True), jnp.max(s2, axis=-1, keepdims=True))
        p1 = jnp.exp2(s1 - m)
        p2 = jnp.exp2(s2 - m)
        l = jnp.sum(p1, axis=-1, keepdims=True) + jnp.sum(p2, axis=-1, keepdims=True)
        o = (_dot(p1.astype(BF16), v_ref[pl.ds(k0, band), :]) + _dot(p2.astype(BF16), vctx)) / l
        o_ref[pl.ds(q0, NAT_QB), :] = o.astype(o_ref.dtype)
        return carry
    lax.fori_loop(0, nb, body, 0, unroll=4)


def nat_attention(qkv, n_heads, cache, tables):
    b, seq, _ = qkv.shape
    ck, cv = cache
    past = ck.shape[1]
    ncls, _, _, band = tables.shape
    _, change = _nat_classes(seq)
    blk = lambda off: pl.BlockSpec((None, seq, HEAD_DIM), lambda bi, h: (bi, 0, off + h))
    cspec = pl.BlockSpec((None, past, HEAD_DIM), lambda bi, h: (bi, 0, h))
    return pl.pallas_call(
        functools.partial(_nat_kernel, seq=seq, change=tuple(change)),
        grid=(b, n_heads),
        in_specs=[blk(0), blk(n_heads), blk(2 * n_heads), cspec, cspec,
                  pl.BlockSpec((ncls, 1, NAT_QB, band), lambda bi, h: (0, h, 0, 0))],
        out_specs=pl.BlockSpec((None, seq, HEAD_DIM), lambda bi, h: (bi, 0, h)),
        out_shape=jax.ShapeDtypeStruct((b, seq, n_heads * HEAD_DIM), BF16),
        compiler_params=_params(2),
        name="nat_attention",
    )(qkv, qkv, qkv, ck.reshape(b, past, n_heads * HEAD_DIM), cv.reshape(b, past, n_heads * HEAD_DIM), tables)


def kernel(x_prompt, x_sample, cache_k0, cache_v0, cache_k1, cache_v1, cache_k2, cache_v2, cache_k3, cache_v3, c, c_ctx, norm_g, w_ada, b_ada, w_ffn_in, w_ffn_out, att_wqkv, att_wo, att_qn, att_kn, win_wqkv, win_wo, win_qn, win_kn, win_sink, diff_wqkv, diff_wo, diff_qn, diff_kn, diff_lq1, diff_lk1, diff_lq2, diff_lk2, diff_subln, nat_wqkv, nat_wo, nat_qn, nat_kn, nat_rpb):
    depth = w_ada.shape[0]
    d = x_prompt.shape[-1]
    bp, sp, _ = x_prompt.shape
    bs, ss, _ = x_sample.shape
    caches = [(cache_k0, cache_v0), (cache_k1, cache_v1), (cache_k2, cache_v2), (cache_k3, cache_v3)]
    kv_heads = [cache_k0.shape[2], cache_k1.shape[2], cache_k2.shape[2], cache_k3.shape[2]]
    wqkv = [w.astype(BF16) for w in (att_wqkv, win_wqkv, diff_wqkv, nat_wqkv)]
    wo = [w.astype(BF16) for w in (att_wo, win_wo, diff_wo, nat_wo)]
    q_cols = [w.shape[0] for w in wo]
    qn = [att_qn, win_qn, diff_qn, nat_qn]
    kn = [att_kn, win_kn, diff_kn, nat_kn]
    w_in = w_ffn_in.astype(BF16)
    w_out = w_ffn_out.astype(BF16)
    lam_vecs = (diff_lq1, diff_lk1, diff_lq2, diff_lk2)

    rows = 8 * (-(-(1 + bs) // 8))
    cond = jnp.zeros((rows, d), F32).at[0].set(c_ctx).at[1:1 + bs].set(c)
    mods = adaln(cond, w_ada, b_ada).reshape(depth, rows, N_MOD, d)

    rope = rope_tables(ss)
    tables = nat_bias_tables(nat_rpb, ss)

    def run(x3, latent):
        b, seq, _ = x3.shape
        mseq = seq if latent else b * seq

        def mod(l, chunk):
            r = mods[l, 1:1 + bs, chunk] if latent else mods[l, 0:1, chunk]
            return r[:, None, :]

        x = x3.reshape(b * seq, d)
        h = normmod(x, norm_g[0, 0], mod(0, 0), mod(0, 1), mseq)
        state = []
        for l in range(depth):
            kind = l % 4
            n_kv = kv_heads[kind]
            a = ffn_in(h, w_in, (l, 0))
            x, h = resid_norm(a, w_out, (l, 0), x, mod(l, 2), 0.5, mseq,
                              norm=(norm_g[l, 1], mod(l, 3), mod(l, 4)))
            kv_cols = (wqkv[kind].shape[1] - q_cols[kind]) // 2
            use_rope = latent and kind != 3
            qkv = qkv_proj(h, wqkv[kind], qn[kind], kn[kind], q_cols[kind], kv_cols,
                           seq if use_rope else mseq, rope if use_rope else None,
                           BF16 if latent else F32)
            qkv = qkv.reshape(b, seq, -1)
            cache = caches[kind] if latent else None
            if kind == 2:
                lam_init = 0.8 - 0.6 * math.exp(-0.3 * l)
                n_heads = q_cols[kind] // (2 * HEAD_DIM)
                o = diff_attention(qkv, n_heads, n_kv, lam_vecs, diff_subln, lam_init, cache)
            elif kind == 3 and latent:
                n_heads = q_cols[kind] // HEAD_DIM
                o = nat_attention(qkv, n_heads, cache, tables)
            elif kind == 1 and latent:
                n_heads = q_cols[kind] // HEAD_DIM
                o = window_attention(qkv, n_heads, n_kv, cache, win_sink)
            else:
                n_heads = q_cols[kind] // HEAD_DIM
                o = attention(qkv, n_heads, n_kv, cache, sink=win_sink if kind == 1 else None)
            if not latent:
                nq = q_cols[kind]
                state.append(qkv[:, :, nq:nq + kv_cols])
                state.append(qkv[:, :, nq + kv_cols:])
            x, h = resid_norm(o.reshape(b * seq, -1), wo[kind][None, None], (0, 0), x, mod(l, 5), 1.0, mseq,
                              norm=(norm_g[l, 2], mod(l, 6), mod(l, 7)))
            a = ffn_in(h, w_in, (l, 1))
            if l + 1 < depth:
                x, h = resid_norm(a, w_out, (l, 1), x, mod(l, 8), 0.5, mseq,
                                  norm=(norm_g[l + 1, 0], mod(l + 1, 0), mod(l + 1, 1)))
            else:
                x = resid_norm(a, w_out, (l, 1), x, mod(l, 8), 0.5, mseq)
        return x.reshape(b, seq, d), state

    yp, st = run(x_prompt, False)
    ys, _ = run(x_sample, True)
    hd = HEAD_DIM
    new_state = [
        st[0].reshape(bp, sp, kv_heads[0], hd), st[1].reshape(bp, sp, kv_heads[0], hd),
        st[2].reshape(bp, sp, kv_heads[1], hd), st[3].reshape(bp, sp, kv_heads[1], hd),
        st[4].reshape(bp, sp, kv_heads[2], 2, hd), st[5].reshape(bp, sp, kv_heads[2], 2 * hd),
        st[6].reshape(bp, sp, kv_heads[3], hd), st[7].reshape(bp, sp, kv_heads[3], hd),
    ]
    return (yp, ys, *new_state)
```

```python
import functools
import math

import jax
import jax.numpy as jnp
import numpy as np
from jax import lax
from jax.experimental import pallas as pl
from jax.experimental.pallas import tpu as pltpu

F32 = jnp.float32
BF16 = jnp.bfloat16

HEAD_DIM = 128
LANES = 128
GRID_W = 64
WINDOW = 128
NA_ROWS = 8
NA_COLS = 16
ROPE_THETA = 10000.0
EPS = 1e-6
N_MOD = 9
NEG_INF = -1e30
LOG2E = math.log2(math.e)
QK_SCALE2 = HEAD_DIM ** -0.5 * LOG2E

V7X_VMEM_BYTES = 64 * 1024 * 1024
VMEM_LIMIT = V7X_VMEM_BYTES - 8 * 1024 * 1024


def _params(n_grid):
    return pltpu.CompilerParams(
        dimension_semantics=("arbitrary",) * n_grid, vmem_limit_bytes=VMEM_LIMIT)


def _tile(n, pref):
    t = min(n, pref)
    while n % t:
        t //= 2
    return t


def _dot(a, b):
    return jnp.dot(a, b, preferred_element_type=F32)


def _dot_nt(a, b):
    return lax.dot_general(a, b, (((1,), (1,)), ((), ())), preferred_element_type=F32)


def _silu(x):
    return x * jax.nn.sigmoid(x)


def _rms(x, n):
    return lax.rsqrt(jnp.sum(x * x, axis=-1, keepdims=True) * (1.0 / n) + EPS)


def _adaln_kernel(c_ref, w_ref, b_ref, o_ref):
    a = _silu(c_ref[...]).astype(BF16)
    o_ref[0] = _dot(a, w_ref[0].astype(BF16)) + b_ref[0]


def adaln(cond, w_ada, b_ada):
    depth, d, n = w_ada.shape
    r = cond.shape[0]
    tn = _tile(n, 512)
    return pl.pallas_call(
        _adaln_kernel,
        grid=(depth, n // tn),
        in_specs=[
            pl.BlockSpec((r, d), lambda l, j: (0, 0)),
            pl.BlockSpec((1, d, tn), lambda l, j: (l, 0, j)),
            pl.BlockSpec((1, 1, tn), lambda l, j: (l, 0, j)),
        ],
        out_specs=pl.BlockSpec((1, r, tn), lambda l, j: (l, 0, j)),
        out_shape=jax.ShapeDtypeStruct((depth, r, n), F32),
        compiler_params=_params(2),
        name="adaln",
    )(cond, w_ada, b_ada.reshape(depth, 1, n))


def _modulated_norm(x, g, shift, scale):
    return (x * _rms(x, x.shape[-1])) * (g * (1.0 + scale)) + shift


def _normmod_kernel(x_ref, g_ref, sh_ref, sc_ref, h_ref):
    h_ref[...] = _modulated_norm(x_ref[...], g_ref[...], sh_ref[0], sc_ref[0]).astype(BF16)


def normmod(x, g, shift, scale, seq):
    t, d = x.shape
    tm = _tile(seq, 512)
    per = seq // tm
    vec = pl.BlockSpec((1, 1, d), lambda i: (i // per, 0, 0))
    return pl.pallas_call(
        _normmod_kernel,
        grid=(t // tm,),
        in_specs=[pl.BlockSpec((tm, d), lambda i: (i, 0)),
                  pl.BlockSpec((1, d), lambda i: (0, 0)), vec, vec],
        out_specs=pl.BlockSpec((tm, d), lambda i: (i, 0)),
        out_shape=jax.ShapeDtypeStruct((t, d), BF16),
        compiler_params=_params(1),
        name="normmod",
    )(x, g.reshape(1, d), shift, scale)


def _ffn_in_kernel(h_ref, wg_ref, wu_ref, a_ref):
    h = h_ref[...]
    gate = _dot(h, wg_ref[...])
    up = _dot(h, wu_ref[...])
    a_ref[...] = (_silu(gate) * up).astype(BF16)


def ffn_in(h, w_in, idx):
    t, d = h.shape
    f = w_in.shape[-1] // 2
    tm = _tile(t, 1024)
    tn = _tile(f, 512)
    nj = f // tn
    l, m = idx
    return pl.pallas_call(
        _ffn_in_kernel,
        grid=(t // tm, nj),
        in_specs=[
            pl.BlockSpec((tm, d), lambda i, j: (i, 0)),
            pl.BlockSpec((None, None, d, tn), lambda i, j: (l, m, 0, j)),
            pl.BlockSpec((None, None, d, tn), lambda i, j: (l, m, 0, j + nj)),
        ],
        out_specs=pl.BlockSpec((tm, tn), lambda i, j: (i, j)),
        out_shape=jax.ShapeDtypeStruct((t, f), BF16),
        compiler_params=_params(2),
        name="ffn_in",
    )(h, w_in, w_in)


RESID_ROW_SPLIT = 2


def _by_parity(step, fn, bufs):
    parity = lax.rem(step, 2)

    @pl.when(parity == 0)
    def _():
        fn(bufs[0], bufs[1])

    @pl.when(parity == 1)
    def _():
        fn(bufs[1], bufs[0])


def _resid_kernel(a_ref, w_ref, x_ref, gate_ref, *rest, coef, with_norm):
    tm = a_ref.shape[0]
    split = RESID_ROW_SPLIT if tm % (16 * RESID_ROW_SPLIT) == 0 else 1
    sub = tm // split
    gate = coef * gate_ref[0]
    for r in range(split):
        rs = slice(r * sub, (r + 1) * sub)
        y = _dot(a_ref[rs, :].astype(BF16), w_ref[...])
        xo = x_ref[rs, :] + gate * y
        if with_norm:
            g_ref, sh_ref, sc_ref, xo_ref, h_ref = rest
            xo_ref[rs, :] = xo
            h_ref[rs, :] = _modulated_norm(xo, g_ref[...], sh_ref[0], sc_ref[0]).astype(BF16)
        else:
            (xo_ref,) = rest
            xo_ref[rs, :] = xo


def resid_norm(a, w, idx, x, gate, coef, seq, norm=None):
    t, k = a.shape
    d = w.shape[-1]
    l, m = idx
    tm = _tile(seq, 256)
    per = seq // tm
    vec = pl.BlockSpec((1, 1, d), lambda i: (i // per, 0, 0))
    row = pl.BlockSpec((tm, d), lambda i: (i, 0))
    in_specs = [
        pl.BlockSpec((tm, k), lambda i: (i, 0)),
        pl.BlockSpec((None, None, k, d), lambda i: (l, m, 0, 0), pipeline_mode=pl.Buffered(1)),
        row, vec,
    ]
    args = [a, w, x, gate]
    if norm is not None:
        g, shift, scale = norm
        in_specs += [pl.BlockSpec((1, d), lambda i: (0, 0)), vec, vec]
        args += [g.reshape(1, d), shift, scale]
        out_specs = [row, row]
        out_shape = [jax.ShapeDtypeStruct((t, d), F32), jax.ShapeDtypeStruct((t, d), BF16)]
    else:
        out_specs = row
        out_shape = jax.ShapeDtypeStruct((t, d), F32)
    return pl.pallas_call(
        functools.partial(_resid_kernel, coef=coef, with_norm=norm is not None),
        grid=(t // tm,),
        in_specs=in_specs,
        out_specs=out_specs,
        out_shape=out_shape,
        compiler_params=_params(1),
        name="resid_norm",
    )(*args)


def _qkv_kernel(h_ref, w_ref, gain_ref, *rest, n_norm, n_col, rope, heads):
    acc_refs = rest[-2:]
    rest = rest[:-2]
    step = pl.program_id(0)
    is_value = lax.rem(jnp.maximum(step - 1, 0), n_col) >= n_norm
    if rope:
        cos_ref, sin_ref, o_ref = rest
    else:
        (o_ref,) = rest

    @pl.when(step == 0)
    def _():
        acc_refs[1][...] = jnp.zeros(acc_refs[1].shape, F32)

    def body(acc_w, acc_r):
        for hh in range(heads):
            sl = slice(hh * HEAD_DIM, (hh + 1) * HEAD_DIM)
            xh = acc_r[:, sl]
            y = (xh * jnp.where(is_value, 1.0, _rms(xh, HEAD_DIM))) * gain_ref[:, sl]
            if rope:
                y = y * cos_ref[...] + pltpu.roll(y, HEAD_DIM // 2, 1) * sin_ref[...]
            o_ref[:, sl] = y.astype(o_ref.dtype)
        acc_w[...] = _dot(h_ref[...], w_ref[...])

    _by_parity(step, body, acc_refs)


def qkv_proj(h, w, qn, kn, n_q, n_k, seq, rope, out_dtype):
    t, d = h.shape
    n = w.shape[1]
    tm = _tile(seq, 1024)
    tn = _tile(math.gcd(n_q, n_k), 512)
    per = seq // tm
    n_norm = (n_q + n_k) // tn
    n_col = n // tn
    n_tiles = (t // tm) * n_col
    gains = jnp.concatenate([jnp.tile(qn.astype(F32), n_q // HEAD_DIM), jnp.tile(kn.astype(F32), n_k // HEAD_DIM),
                             jnp.ones((n - n_q - n_k,), F32)]).reshape(1, n)
    cur = lambda s: jnp.minimum(s, n_tiles - 1)
    lag = lambda s: jnp.maximum(s - 1, 0)
    in_specs = [
        pl.BlockSpec((tm, d), lambda s: (cur(s) // n_col, 0)),
        pl.BlockSpec((d, tn), lambda s: (0, cur(s) % n_col)),
        pl.BlockSpec((1, tn), lambda s: (0, lag(s) % n_col)),
    ]
    args = [h, w, gains]
    if rope is not None:
        cos, sin = rope
        tab = pl.BlockSpec((None, tm, HEAD_DIM),
                           lambda s: (jnp.where(lag(s) % n_col >= n_norm, 1, 0), (lag(s) // n_col) % per, 0))
        in_specs += [tab, tab]
        args += [jnp.stack([cos, jnp.ones_like(cos)]), jnp.stack([sin, jnp.zeros_like(sin)])]
    return pl.pallas_call(
        functools.partial(_qkv_kernel, n_norm=n_norm, n_col=n_col, rope=rope is not None,
                          heads=tn // HEAD_DIM),
        grid=(n_tiles + 1,),
        in_specs=in_specs,
        out_specs=pl.BlockSpec((tm, tn), lambda s: (lag(s) // n_col, lag(s) % n_col)),
        out_shape=jax.ShapeDtypeStruct((t, n), out_dtype),
        scratch_shapes=[pltpu.VMEM((tm, tn), F32), pltpu.VMEM((tm, tn), F32)],
        compiler_params=_params(1),
        name="qkv_proj",
    )(*args)


def rope_tables(seq):
    t = jnp.arange(seq)
    row = (t // GRID_W).astype(F32)
    col = (t % GRID_W).astype(F32)
    nf = HEAD_DIM // 4
    inv = ROPE_THETA ** (-jnp.arange(nf, dtype=F32) / nf)
    ang = jnp.concatenate([row[:, None] * inv, col[:, None] * inv], axis=-1)
    cos, sin = jnp.cos(ang), jnp.sin(ang)
    return jnp.concatenate([cos, cos], axis=-1), jnp.concatenate([-sin, sin], axis=-1)


ATTN_TQ = 256
ATTN_TK = 1024
PASS_UNROLL = 2


def _stack_heads(q, idx):
    parts = [q[:, i * HEAD_DIM:(i + 1) * HEAD_DIM] for i in idx]
    return parts[0] if len(parts) == 1 else jnp.concatenate(parts, axis=0)


def _lane_slabs(x):
    return [x[:, i * LANES:(i + 1) * LANES] for i in range(x.shape[1] // LANES)]


def _tile_lanes(x, width):
    n = width // LANES
    return x if n == 1 else jnp.concatenate([x] * n, axis=1)


def _tile_rows(x, n):
    return x if n == 1 else jnp.concatenate([x] * n, axis=0)


def _logits_pass(qs, kc, s_out, m_ref, bias=None):
    s = _dot_nt(qs, kc) * QK_SCALE2
    if bias is not None:
        s = s + bias
    s_out[...] = s
    m_ref[...] = jnp.maximum(m_ref[...], functools.reduce(jnp.maximum, _lane_slabs(s)))


def _finish_max(m_ref):
    m_ref[...] = jnp.broadcast_to(jnp.max(m_ref[...], axis=-1, keepdims=True), m_ref.shape)


def _with_ones(v):
    return jnp.concatenate([v, jnp.ones((v.shape[0], LANES), v.dtype)], axis=1)


def _value_pass(s_in, vc, m_ref, l_ref, acc_ref):
    s = s_in[...]
    p = jnp.exp2((s - _tile_lanes(m_ref[...], s.shape[1])).astype(BF16))
    if l_ref is None:
        acc_ref[...] += _dot(p, _with_ones(vc))
    else:
        l_ref[...] += functools.reduce(jnp.add, _lane_slabs(p.astype(F32)))
        acc_ref[...] += _dot(p, vc)


def _row_sum(l_ref):
    return jnp.sum(l_ref[...], axis=-1, keepdims=True)


def _sink_rows(sink_ref, groups, tq):
    sk = sink_ref[0]
    parts = [jnp.broadcast_to(sk[g:g + 1, :], (tq, LANES)) for g in range(groups)]
    return parts[0] if groups == 1 else jnp.concatenate(parts, axis=0)


def _attn_kernel(*refs, groups, tq, tk, seq, has_cache, has_sink):
    refs = list(refs)
    q_ref, k_ref, v_ref = refs[:3]
    pos = 3
    if has_cache:
        ck_ref, cv_ref = refs[pos:pos + 2]
        pos += 2
    if has_sink:
        sink_ref = refs[pos]
        pos += 1
    o_ref, s_ref = refs[pos:pos + 2]
    pos += 2
    if has_cache:
        sc_ref = refs[pos]
        pos += 1
    m_ref, acc_ref = refs[pos:]
    rows = groups * tq
    n_chunks = seq // tk

    qs = _stack_heads(q_ref[...].astype(BF16), range(groups))
    if has_sink:
        sink2 = _sink_rows(sink_ref, groups, tq)
        m_ref[...] = sink2
    else:
        m_ref[...] = jnp.full((rows, LANES), NEG_INF, F32)
    acc_ref[...] = jnp.zeros(acc_ref.shape, F32)

    def p1(c, carry):
        off = pl.multiple_of(c * tk, tk)
        _logits_pass(qs, k_ref[pl.ds(off, tk), :].astype(BF16), s_ref.at[c], m_ref)
        return carry
    lax.fori_loop(0, n_chunks, p1, 0, unroll=PASS_UNROLL)
    if has_cache:
        _logits_pass(qs, ck_ref[...].astype(BF16), sc_ref, m_ref)
    _finish_max(m_ref)

    def p2(c, carry):
        off = pl.multiple_of(c * tk, tk)
        _value_pass(s_ref.at[c], v_ref[pl.ds(off, tk), :].astype(BF16), m_ref, None, acc_ref)
        return carry
    lax.fori_loop(0, n_chunks, p2, 0, unroll=PASS_UNROLL)
    if has_cache:
        _value_pass(sc_ref, cv_ref[...].astype(BF16), m_ref, None, acc_ref)

    acc = acc_ref[...]
    l = acc[:, HEAD_DIM:]
    if has_sink:
        l = l + jnp.exp2(sink2 - m_ref[...])
    o = acc[:, :HEAD_DIM] / l
    for g in range(groups):
        o_ref[:, g * HEAD_DIM:(g + 1) * HEAD_DIM] = o[g * tq:(g + 1) * tq].astype(o_ref.dtype)


def _sink_arg(sink, n_kv, groups):
    sink2 = sink.astype(F32).reshape(n_kv, groups, 1) * LOG2E
    return jnp.broadcast_to(sink2, (n_kv, groups, LANES))


def attention(qkv, n_heads, n_kv, cache=None, sink=None):
    b, seq, _ = qkv.shape
    groups = n_heads // n_kv
    tq = _tile(seq, ATTN_TQ)
    tk = _tile(seq, ATTN_TK)
    gw = groups * HEAD_DIM
    rows = groups * tq
    in_specs = [
        pl.BlockSpec((None, tq, gw), lambda bi, h, qi: (bi, qi, h)),
        pl.BlockSpec((None, seq, HEAD_DIM), lambda bi, h, qi: (bi, 0, n_heads + h)),
        pl.BlockSpec((None, seq, HEAD_DIM), lambda bi, h, qi: (bi, 0, n_heads + n_kv + h)),
    ]
    args = [qkv, qkv, qkv]
    scratch = [pltpu.VMEM((seq // tk, rows, tk), F32)]
    if cache is not None:
        ck, cv = cache
        past = ck.shape[1]
        spec = pl.BlockSpec((None, past, HEAD_DIM), lambda bi, h, qi: (bi, 0, h))
        in_specs += [spec, spec]
        args += [ck.reshape(b, past, n_kv * HEAD_DIM), cv.reshape(b, past, n_kv * HEAD_DIM)]
        scratch.append(pltpu.VMEM((rows, past), F32))
    if sink is not None:
        in_specs.append(pl.BlockSpec((1, groups, LANES), lambda bi, h, qi: (h, 0, 0)))
        args.append(_sink_arg(sink, n_kv, groups))
    scratch += [pltpu.VMEM((rows, LANES), F32), pltpu.VMEM((rows, HEAD_DIM + LANES), F32)]
    return pl.pallas_call(
        functools.partial(_attn_kernel, groups=groups, tq=tq, tk=tk, seq=seq,
                          has_cache=cache is not None, has_sink=sink is not None),
        grid=(b, n_kv, seq // tq),
        in_specs=in_specs,
        out_specs=pl.BlockSpec((None, tq, gw), lambda bi, h, qi: (bi, qi, h)),
        out_shape=jax.ShapeDtypeStruct((b, seq, n_heads * HEAD_DIM), BF16),
        scratch_shapes=scratch,
        compiler_params=_params(3),
        name="attention",
    )(*args)


WIN_QB = 128


def _window_masks(seq):
    band = WIN_QB + 2 * WINDOW
    out = []
    for q0 in (0, WIN_QB, seq - WIN_QB):
        start = int(np.clip(q0 - WINDOW, 0, seq - band))
        qpos = q0 + np.arange(WIN_QB)[:, None]
        kpos = start + np.arange(band)[None, :]
        out.append(np.where(np.abs(qpos - kpos) <= WINDOW, 0.0, NEG_INF))
    return jnp.asarray(np.stack(out), F32)


def _win_kernel(q_ref, k_ref, v_ref, ck_ref, cv_ref, sink_ref, wmask_ref, o_ref, *, groups, seq):
    tq = WIN_QB
    band = tq + 2 * WINDOW
    nb = seq // tq
    past = ck_ref.shape[0]
    kctx = ck_ref[...].astype(BF16)
    vctx1 = _with_ones(cv_ref[...].astype(BF16))
    sink2 = _sink_rows(sink_ref, groups, tq)

    def body(j, carry):
        q0 = pl.multiple_of(j * tq, tq)
        qs = _stack_heads(q_ref[pl.ds(q0, tq), :], range(groups))
        start = pl.multiple_of(jnp.clip(q0 - WINDOW, 0, seq - band), WINDOW)
        case = jnp.where(j == 0, 0, jnp.where(j == nb - 1, 2, 1))
        s1 = _dot_nt(qs, k_ref[pl.ds(start, band), :]) * QK_SCALE2 + _tile_rows(wmask_ref[case], groups)
        s2 = _dot_nt(qs, kctx) * QK_SCALE2
        mp = functools.reduce(jnp.maximum, _lane_slabs(s1) + _lane_slabs(s2) + [sink2])
        m = jnp.broadcast_to(jnp.max(mp, axis=-1, keepdims=True), mp.shape)
        p1 = jnp.exp2((s1 - _tile_lanes(m, band)).astype(BF16))
        p2 = jnp.exp2((s2 - _tile_lanes(m, past)).astype(BF16))
        acc = _dot(p1, _with_ones(v_ref[pl.ds(start, band), :])) + _dot(p2, vctx1)
        o = acc[:, :HEAD_DIM] / (acc[:, HEAD_DIM:] + jnp.exp2(sink2 - m))
        for g in range(groups):
            o_ref[pl.ds(q0, tq), g * HEAD_DIM:(g + 1) * HEAD_DIM] = o[g * tq:(g + 1) * tq].astype(o_ref.dtype)
        return carry
    lax.fori_loop(0, nb, body, 0, unroll=4)


def window_attention(qkv, n_heads, n_kv, cache, sink):
    b, seq, _ = qkv.shape
    groups = n_heads // n_kv
    gw = groups * HEAD_DIM
    band = WIN_QB + 2 * WINDOW
    assert seq % (2 * WIN_QB) == 0 and seq >= band + WIN_QB and qkv.dtype == BF16
    ck, cv = cache
    past = ck.shape[1]
    cspec = pl.BlockSpec((None, past, HEAD_DIM), lambda bi, h: (bi, 0, h))
    return pl.pallas_call(
        functools.partial(_win_kernel, groups=groups, seq=seq),
        grid=(b, n_kv),
        in_specs=[
            pl.BlockSpec((None, seq, gw), lambda bi, h: (bi, 0, h)),
            pl.BlockSpec((None, seq, HEAD_DIM), lambda bi, h: (bi, 0, n_heads + h)),
            pl.BlockSpec((None, seq, HEAD_DIM), lambda bi, h: (bi, 0, n_heads + n_kv + h)),
            cspec, cspec,
            pl.BlockSpec((1, groups, LANES), lambda bi, h: (h, 0, 0)),
            pl.BlockSpec((3, WIN_QB, band), lambda bi, h: (0, 0, 0)),
        ],
        out_specs=pl.BlockSpec((None, seq, gw), lambda bi, h: (bi, 0, h)),
        out_shape=jax.ShapeDtypeStruct((b, seq, n_heads * HEAD_DIM), BF16),
        compiler_params=_params(2),
        name="window_attention",
    )(qkv, qkv, qkv, ck.reshape(b, past, n_kv * HEAD_DIM), cv.reshape(b, past, n_kv * HEAD_DIM),
      _sink_arg(sink, n_kv, groups), _window_masks(seq))


def _diff_kernel(*refs, groups, tq, tk, seq, has_cache, lam_init):
    refs = list(refs)
    q_ref, k_ref, v_ref = refs[:3]
    pos = 3
    if has_cache:
        ck_ref, cv_ref = refs[pos:pos + 2]
        pos += 2
    lq1_ref, lk1_ref, lq2_ref, lk2_ref, sub_ref = refs[pos:pos + 5]
    pos += 5
    o_ref, s_ref = refs[pos:pos + 2]
    pos += 2
    if has_cache:
        sc_ref = refs[pos]
        pos += 1
    m_ref, l_ref, acc_ref = refs[pos:]
    rows = groups * tq
    dv = 2 * HEAD_DIM
    n_chunks = seq // tk

    q = q_ref[...].astype(BF16)
    outs = []
    for i in range(2):
        cols = slice(i * HEAD_DIM, (i + 1) * HEAD_DIM)
        qs = _stack_heads(q, [2 * g + i for g in range(groups)])
        m_ref[...] = jnp.full((rows, LANES), NEG_INF, F32)
        l_ref[...] = jnp.zeros((rows, LANES), F32)
        acc_ref[...] = jnp.zeros((rows, dv), F32)

        def p1(c, carry, qs=qs, cols=cols):
            off = pl.multiple_of(c * tk, tk)
            _logits_pass(qs, k_ref[pl.ds(off, tk), cols].astype(BF16), s_ref.at[c], m_ref)
            return carry
        lax.fori_loop(0, n_chunks, p1, 0, unroll=PASS_UNROLL)
        if has_cache:
            _logits_pass(qs, ck_ref[:, cols].astype(BF16), sc_ref, m_ref)
        _finish_max(m_ref)

        def p2(c, carry):
            off = pl.multiple_of(c * tk, tk)
            _value_pass(s_ref.at[c], v_ref[pl.ds(off, tk), :].astype(BF16), m_ref, l_ref, acc_ref)
            return carry
        lax.fori_loop(0, n_chunks, p2, 0, unroll=PASS_UNROLL)
        if has_cache:
            _value_pass(sc_ref, cv_ref[...].astype(BF16), m_ref, l_ref, acc_ref)
        outs.append(acc_ref[...] / _row_sum(l_ref))

    e1 = jnp.exp(jnp.sum(lq1_ref[...] * lk1_ref[...], axis=-1, keepdims=True))
    e2 = jnp.exp(jnp.sum(lq2_ref[...] * lk2_ref[...], axis=-1, keepdims=True))
    lam = e1 - e2 + lam_init
    o = outs[0] - lam * outs[1]
    o = ((o * _rms(o, dv)) * sub_ref[...]) * (1.0 - lam_init)
    for g in range(groups):
        o_ref[:, g * dv:(g + 1) * dv] = o[g * tq:(g + 1) * tq].astype(o_ref.dtype)


def diff_attention(qkv, n_heads, n_kv, lam_vecs, subln, lam_init, cache=None):
    b, seq, _ = qkv.shape
    groups = n_heads // n_kv
    dv = 2 * HEAD_DIM
    tq = _tile(seq, ATTN_TQ)
    tk = _tile(seq, ATTN_TK)
    gw = groups * dv
    rows = groups * tq
    in_specs = [
        pl.BlockSpec((None, tq, gw), lambda bi, h, qi: (bi, qi, h)),
        pl.BlockSpec((None, seq, dv), lambda bi, h, qi: (bi, 0, n_heads + h)),
        pl.BlockSpec((None, seq, dv), lambda bi, h, qi: (bi, 0, n_heads + n_kv + h)),
    ]
    args = [qkv, qkv, qkv]
    scratch = [pltpu.VMEM((seq // tk, rows, tk), F32)]
    if cache is not None:
        ck, cv = cache
        past = ck.shape[1]
        spec = pl.BlockSpec((None, past, dv), lambda bi, h, qi: (bi, 0, h))
        in_specs += [spec, spec]
        args += [ck.reshape(b, past, n_kv * dv), cv.reshape(b, past, n_kv * dv)]
        scratch.append(pltpu.VMEM((rows, past), F32))
    vec = pl.BlockSpec((1, HEAD_DIM), lambda bi, h, qi: (0, 0))
    in_specs += [vec] * 4 + [pl.BlockSpec((1, dv), lambda bi, h, qi: (0, 0))]
    args += [v.astype(F32).reshape(1, HEAD_DIM) for v in lam_vecs] + [subln.astype(F32).reshape(1, dv)]
    scratch += [pltpu.VMEM((rows, LANES), F32), pltpu.VMEM((rows, LANES), F32),
                pltpu.VMEM((rows, dv), F32)]
    return pl.pallas_call(
        functools.partial(_diff_kernel, groups=groups, tq=tq, tk=tk, seq=seq,
                          has_cache=cache is not None, lam_init=lam_init),
        grid=(b, n_kv, seq // tq),
        in_specs=in_specs,
        out_specs=pl.BlockSpec((None, tq, gw), lambda bi, h, qi: (bi, qi, h)),
        out_shape=jax.ShapeDtypeStruct((b, seq, n_heads * dv), BF16),
        scratch_shapes=scratch,
        compiler_params=_params(3),
        name="diff_attention",
    )(*args)


NAT_QB = 256


def _nat_geometry(seq):
    rows = seq // GRID_W
    kr_n = min(NA_ROWS, rows)
    q_rows = NAT_QB // GRID_W
    span = min(kr_n + q_rows - 1, rows)
    return rows, kr_n, q_rows, span


def _nat_band_start(j, seq):
    rows, kr_n, q_rows, span = _nat_geometry(seq)
    clip = jnp.clip if isinstance(j, jax.Array) else np.clip
    mini = jnp.minimum if isinstance(j, jax.Array) else np.minimum
    return mini(clip(j * q_rows - kr_n // 2, 0, rows - kr_n), rows - span)


def _nat_block_pattern(j, seq):
    rows, kr_n, q_rows, span = _nat_geometry(seq)
    qr = j * q_rows + np.arange(q_rows)
    rs = np.clip(qr - kr_n // 2, 0, rows - kr_n)
    kr = int(_nat_band_start(j, seq)) + np.arange(span)
    ok_r = (kr[None, :] >= rs[:, None]) & (kr[None, :] < rs[:, None] + kr_n)
    dr = np.clip(kr[None, :] - qr[:, None] + NA_ROWS - 1, 0, 2 * NA_ROWS - 2)
    return ok_r, dr


def _nat_classes(seq):
    nb = seq // NAT_QB
    reps, change = [], []
    for j in range(nb):
        ok_r, dr = _nat_block_pattern(j, seq)
        key = (ok_r.tobytes(), np.where(ok_r, dr, 0).tobytes())
        if not reps or key != reps[-1][0]:
            assert all(key != k for k, _ in reps), "window patterns must be contiguous runs of blocks"
            reps.append((key, j))
            change.append(j)
    return [j for _, j in reps], change[1:]


def nat_bias_tables(rpb, seq):
    _, _, q_rows, span = _nat_geometry(seq)
    n_heads = rpb.shape[0]
    cl = np.arange(GRID_W)
    cs = np.clip(cl - NA_COLS // 2, 0, GRID_W - NA_COLS)
    ok_c = (cl[None, :] >= cs[:, None]) & (cl[None, :] < cs[:, None] + NA_COLS)
    dc = np.clip(cl[None, :] - cl[:, None] + NA_COLS - 1, 0, 2 * NA_COLS - 2)
    by_col = rpb.astype(F32)[:, :, dc] * LOG2E
    reps, _ = _nat_classes(seq)
    tabs = []
    for j in reps:
        ok_r, dr = _nat_block_pattern(j, seq)
        t = by_col[:, dr]
        ok = ok_r[:, :, None, None] & ok_c[None, None]
        t = jnp.where(ok[None], t, NEG_INF)
        tabs.append(t.transpose(0, 1, 3, 2, 4).reshape(n_heads, q_rows * GRID_W, span * GRID_W))
    return jnp.stack(tabs, axis=0)


def _nat_kernel(q_ref, k_ref, v_ref, ck_ref, cv_ref, t_ref, o_ref, *, seq, change):
    nb = seq // NAT_QB
    band = t_ref.shape[-1]
    kctx = ck_ref[...].astype(BF16)
    vctx1 = _with_ones(cv_ref[...].astype(BF16))

    def body(j, carry):
        q0 = pl.multiple_of(j * NAT_QB, NAT_QB)
        q = q_ref[pl.ds(q0, NAT_QB), :]
        k0 = pl.multiple_of(_nat_band_start(j, seq) * GRID_W, GRID_W)
        cls = functools.reduce(lambda a, cp: a + jnp.where(j >= cp, 1, 0), change, 0)
        s1 = _dot_nt(q, k_ref[pl.ds(k0, band), :]) * QK_SCALE2 + t_ref[cls, 0]
        s2 = _dot_nt(q, kctx) * QK_SCALE2
        m = jnp.maximum(jnp.max(s1, axis=-1, keepdims=True), jnp.max(s2, axis=-1, keepdims=True))
        p1 = jnp.exp2((s1 - m).astype(BF16))
        p2 = jnp.exp2((s2 - m).astype(BF16))
        acc = _dot(p1, _with_ones(v_ref[pl.ds(k0, band), :])) + _dot(p2, vctx1)
        o = acc[:, :HEAD_DIM] / acc[:, HEAD_DIM:]
        o_ref[pl.ds(q0, NAT_QB), :] = o.astype(o_ref.dtype)
        return carry
    lax.fori_loop(0, nb, body, 0, unroll=4)


def nat_attention(qkv, n_heads, cache, tables):
    b, seq, _ = qkv.shape
    ck, cv = cache
    past = ck.shape[1]
    ncls, _, _, band = tables.shape
    _, change = _nat_classes(seq)
    blk = lambda off: pl.BlockSpec((None, seq, HEAD_DIM), lambda bi, h: (bi, 0, off + h))
    cspec = pl.BlockSpec((None, past, HEAD_DIM), lambda bi, h: (bi, 0, h))
    return pl.pallas_call(
        functools.partial(_nat_kernel, seq=seq, change=tuple(change)),
        grid=(b, n_heads),
        in_specs=[blk(0), blk(n_heads), blk(2 * n_heads), cspec, cspec,
                  pl.BlockSpec((ncls, 1, NAT_QB, band), lambda bi, h: (0, h, 0, 0))],
        out_specs=pl.BlockSpec((None, seq, HEAD_DIM), lambda bi, h: (bi, 0, h)),
        out_shape=jax.ShapeDtypeStruct((b, seq, n_heads * HEAD_DIM), BF16),
        compiler_params=_params(2),
        name="nat_attention",
    )(qkv, qkv, qkv, ck.reshape(b, past, n_heads * HEAD_DIM), cv.reshape(b, past, n_heads * HEAD_DIM), tables)


def kernel(x_prompt, x_sample, cache_k0, cache_v0, cache_k1, cache_v1, cache_k2, cache_v2, cache_k3, cache_v3, c, c_ctx, norm_g, w_ada, b_ada, w_ffn_in, w_ffn_out, att_wqkv, att_wo, att_qn, att_kn, win_wqkv, win_wo, win_qn, win_kn, win_sink, diff_wqkv, diff_wo, diff_qn, diff_kn, diff_lq1, diff_lk1, diff_lq2, diff_lk2, diff_subln, nat_wqkv, nat_wo, nat_qn, nat_kn, nat_rpb):
    depth = w_ada.shape[0]
    d = x_prompt.shape[-1]
    bp, sp, _ = x_prompt.shape
    bs, ss, _ = x_sample.shape
    caches = [(cache_k0, cache_v0), (cache_k1, cache_v1), (cache_k2, cache_v2), (cache_k3, cache_v3)]
    kv_heads = [cache_k0.shape[2], cache_k1.shape[2], cache_k2.shape[2], cache_k3.shape[2]]
    wqkv = [w.astype(BF16) for w in (att_wqkv, win_wqkv, diff_wqkv, nat_wqkv)]
    wo = [w.astype(BF16) for w in (att_wo, win_wo, diff_wo, nat_wo)]
    q_cols = [w.shape[0] for w in wo]
    qn = [att_qn, win_qn, diff_qn, nat_qn]
    kn = [att_kn, win_kn, diff_kn, nat_kn]
    w_in = w_ffn_in.astype(BF16)
    w_out = w_ffn_out.astype(BF16)
    lam_vecs = (diff_lq1, diff_lk1, diff_lq2, diff_lk2)

    rows = 8 * (-(-(1 + bs) // 8))
    cond = jnp.zeros((rows, d), F32).at[0].set(c_ctx).at[1:1 + bs].set(c)
    mods = adaln(cond, w_ada, b_ada).reshape(depth, rows, N_MOD, d)

    rope = rope_tables(ss)
    tables = nat_bias_tables(nat_rpb, ss)

    def run(x3, latent):
        b, seq, _ = x3.shape
        mseq = seq if latent else b * seq

        def mod(l, chunk):
            r = mods[l, 1:1 + bs, chunk] if latent else mods[l, 0:1, chunk]
            return r[:, None, :]

        x = x3.reshape(b * seq, d)
        h = normmod(x, norm_g[0, 0], mod(0, 0), mod(0, 1), mseq)
        state = []
        for l in range(depth):
            kind = l % 4
            n_kv = kv_heads[kind]
            a = ffn_in(h, w_in, (l, 0))
            x, h = resid_norm(a, w_out, (l, 0), x, mod(l, 2), 0.5, mseq,
                              norm=(norm_g[l, 1], mod(l, 3), mod(l, 4)))
            kv_cols = (wqkv[kind].shape[1] - q_cols[kind]) // 2
            use_rope = latent and kind != 3
            qkv = qkv_proj(h, wqkv[kind], qn[kind], kn[kind], q_cols[kind], kv_cols,
                           seq if use_rope else mseq, rope if use_rope else None,
                           BF16 if latent else F32)
            qkv = qkv.reshape(b, seq, -1)
            cache = caches[kind] if latent else None
            if kind == 2:
                lam_init = 0.8 - 0.6 * math.exp(-0.3 * l)
                n_heads = q_cols[kind] // (2 * HEAD_DIM)
                o = diff_attention(qkv, n_heads, n_kv, lam_vecs, diff_subln, lam_init, cache)
            elif kind == 3 and latent:
                n_heads = q_cols[kind] // HEAD_DIM
                o = nat_attention(qkv, n_heads, cache, tables)
            elif kind == 1 and latent:
                n_heads = q_cols[kind] // HEAD_DIM
                o = window_attention(qkv, n_heads, n_kv, cache, win_sink)
            else:
                n_heads = q_cols[kind] // HEAD_DIM
                o = attention(qkv, n_heads, n_kv, cache, sink=win_sink if kind == 1 else None)
            if not latent:
                nq = q_cols[kind]
                state.append(qkv[:, :, nq:nq + kv_cols])
                state.append(qkv[:, :, nq + kv_cols:])
            x, h = resid_norm(o.reshape(b * seq, -1), wo[kind][None, None], (0, 0), x, mod(l, 5), 1.0, mseq,
                              norm=(norm_g[l, 2], mod(l, 6), mod(l, 7)))
            a = ffn_in(h, w_in, (l, 1))
            if l + 1 < depth:
                x, h = resid_norm(a, w_out, (l, 1), x, mod(l, 8), 0.5, mseq,
                                  norm=(norm_g[l + 1, 0], mod(l + 1, 0), mod(l + 1, 1)))
            else:
                x = resid_norm(a, w_out, (l, 1), x, mod(l, 8), 0.5, mseq)
        return x.reshape(b, seq, d), state

    yp, st = run(x_prompt, False)
    ys, _ = run(x_sample, True)
    hd = HEAD_DIM
    new_state = [
        st[0].reshape(bp, sp, kv_heads[0], hd), st[1].reshape(bp, sp, kv_heads[0], hd),
        st[2].reshape(bp, sp, kv_heads[1], hd), st[3].reshape(bp, sp, kv_heads[1], hd),
        st[4].reshape(bp, sp, kv_heads[2], 2, hd), st[5].reshape(bp, sp, kv_heads[2], 2 * hd),
        st[6].reshape(bp, sp, kv_heads[3], hd), st[7].reshape(bp, sp, kv_heads[3], hd),
    ]
    return (yp, ys, *new_state)
```

```python
import functools
import math

import jax
import jax.numpy as jnp
import numpy as np
from jax import lax
from jax.experimental import pallas as pl
from jax.experimental.pallas import tpu as pltpu

F32 = jnp.float32
BF16 = jnp.bfloat16

HEAD_DIM = 128
LANES = 128
GRID_W = 64
WINDOW = 128
NA_ROWS = 8
NA_COLS = 16
ROPE_THETA = 10000.0
EPS = 1e-6
N_MOD = 9
NEG_INF = -1e30
LOG2E = math.log2(math.e)
QK_SCALE2 = HEAD_DIM ** -0.5 * LOG2E

V7X_VMEM_BYTES = 64 * 1024 * 1024
VMEM_LIMIT = V7X_VMEM_BYTES - 8 * 1024 * 1024


def _params(n_grid):
    return pltpu.CompilerParams(
        dimension_semantics=("arbitrary",) * n_grid, vmem_limit_bytes=VMEM_LIMIT)


def _tile(n, pref):
    t = min(n, pref)
    while n % t:
        t //= 2
    return t


def _dot(a, b):
    return jnp.dot(a, b, preferred_element_type=F32)


def _dot_nt(a, b):
    return lax.dot_general(a, b, (((1,), (1,)), ((), ())), preferred_element_type=F32)


def _silu(x):
    return x * jax.nn.sigmoid(x)


def _rms(x, n):
    return lax.rsqrt(jnp.sum(x * x, axis=-1, keepdims=True) * (1.0 / n) + EPS)


def _adaln_kernel(c_ref, w_ref, b_ref, o_ref):
    a = _silu(c_ref[...]).astype(BF16)
    o_ref[0] = _dot(a, w_ref[0].astype(BF16)) + b_ref[0]


def adaln(cond, w_ada, b_ada):
    depth, d, n = w_ada.shape
    r = cond.shape[0]
    tn = _tile(n, 512)
    return pl.pallas_call(
        _adaln_kernel,
        grid=(depth, n // tn),
        in_specs=[
            pl.BlockSpec((r, d), lambda l, j: (0, 0)),
            pl.BlockSpec((1, d, tn), lambda l, j: (l, 0, j)),
            pl.BlockSpec((1, 1, tn), lambda l, j: (l, 0, j)),
        ],
        out_specs=pl.BlockSpec((1, r, tn), lambda l, j: (l, 0, j)),
        out_shape=jax.ShapeDtypeStruct((depth, r, n), F32),
        compiler_params=_params(2),
        name="adaln",
    )(cond, w_ada, b_ada.reshape(depth, 1, n))


def _modulated_norm(x, g, shift, scale):
    return (x * _rms(x, x.shape[-1])) * (g * (1.0 + scale)) + shift


def _normmod_kernel(x_ref, g_ref, sh_ref, sc_ref, h_ref):
    h_ref[...] = _modulated_norm(x_ref[...], g_ref[...], sh_ref[0], sc_ref[0]).astype(BF16)


def normmod(x, g, shift, scale, seq):
    t, d = x.shape
    tm = _tile(seq, 512)
    per = seq // tm
    vec = pl.BlockSpec((1, 1, d), lambda i: (i // per, 0, 0))
    return pl.pallas_call(
        _normmod_kernel,
        grid=(t // tm,),
        in_specs=[pl.BlockSpec((tm, d), lambda i: (i, 0)),
                  pl.BlockSpec((1, d), lambda i: (0, 0)), vec, vec],
        out_specs=pl.BlockSpec((tm, d), lambda i: (i, 0)),
        out_shape=jax.ShapeDtypeStruct((t, d), BF16),
        compiler_params=_params(1),
        name="normmod",
    )(x, g.reshape(1, d), shift, scale)


def _ffn_in_kernel(h_ref, wg_ref, wu_ref, a_ref):
    h = h_ref[...]
    gate = _dot(h, wg_ref[...])
    up = _dot(h, wu_ref[...])
    a_ref[...] = (_silu(gate) * up).astype(BF16)


def ffn_in(h, w_in, idx):
    t, d = h.shape
    f = w_in.shape[-1] // 2
    tm = _tile(t, 1024)
    tn = _tile(f, 512)
    nj = f // tn
    l, m = idx
    return pl.pallas_call(
        _ffn_in_kernel,
        grid=(t // tm, nj),
        in_specs=[
            pl.BlockSpec((tm, d), lambda i, j: (i, 0)),
            pl.BlockSpec((None, None, d, tn), lambda i, j: (l, m, 0, j)),
            pl.BlockSpec((None, None, d, tn), lambda i, j: (l, m, 0, j + nj)),
        ],
        out_specs=pl.BlockSpec((tm, tn), lambda i, j: (i, j)),
        out_shape=jax.ShapeDtypeStruct((t, f), BF16),
        compiler_params=_params(2),
        name="ffn_in",
    )(h, w_in, w_in)


RESID_ROW_SPLIT = 2


def _by_parity(step, fn, bufs):
    parity = lax.rem(step, 2)

    @pl.when(parity == 0)
    def _():
        fn(bufs[0], bufs[1])

    @pl.when(parity == 1)
    def _():
        fn(bufs[1], bufs[0])


def _resid_kernel(a_ref, w_ref, x_ref, gate_ref, *rest, coef, with_norm):
    tm = a_ref.shape[0]
    split = RESID_ROW_SPLIT if tm % (16 * RESID_ROW_SPLIT) == 0 else 1
    sub = tm // split
    gate = coef * gate_ref[0]
    for r in range(split):
        rs = slice(r * sub, (r + 1) * sub)
        y = _dot(a_ref[rs, :].astype(BF16), w_ref[...])
        xo = x_ref[rs, :] + gate * y
        if with_norm:
            g_ref, sh_ref, sc_ref, xo_ref, h_ref = rest
            xo_ref[rs, :] = xo
            h_ref[rs, :] = _modulated_norm(xo, g_ref[...], sh_ref[0], sc_ref[0]).astype(BF16)
        else:
            (xo_ref,) = rest
            xo_ref[rs, :] = xo


def resid_norm(a, w, idx, x, gate, coef, seq, norm=None):
    t, k = a.shape
    d = w.shape[-1]
    l, m = idx
    tm = _tile(seq, 256)
    per = seq // tm
    vec = pl.BlockSpec((1, 1, d), lambda i: (i // per, 0, 0))
    row = pl.BlockSpec((tm, d), lambda i: (i, 0))
    in_specs = [
        pl.BlockSpec((tm, k), lambda i: (i, 0)),
        pl.BlockSpec((None, None, k, d), lambda i: (l, m, 0, 0), pipeline_mode=pl.Buffered(1)),
        row, vec,
    ]
    args = [a, w, x, gate]
    if norm is not None:
        g, shift, scale = norm
        in_specs += [pl.BlockSpec((1, d), lambda i: (0, 0)), vec, vec]
        args += [g.reshape(1, d), shift, scale]
        out_specs = [row, row]
        out_shape = [jax.ShapeDtypeStruct((t, d), F32), jax.ShapeDtypeStruct((t, d), BF16)]
    else:
        out_specs = row
        out_shape = jax.ShapeDtypeStruct((t, d), F32)
    return pl.pallas_call(
        functools.partial(_resid_kernel, coef=coef, with_norm=norm is not None),
        grid=(t // tm,),
        in_specs=in_specs,
        out_specs=out_specs,
        out_shape=out_shape,
        compiler_params=_params(1),
        name="resid_norm",
    )(*args)


def _qkv_kernel(h_ref, w_ref, gain_ref, *rest, n_norm, n_col, rope, heads):
    acc_refs = rest[-2:]
    rest = rest[:-2]
    step = pl.program_id(0)
    is_value = lax.rem(jnp.maximum(step - 1, 0), n_col) >= n_norm
    if rope:
        cos_ref, sin_ref, o_ref = rest
    else:
        (o_ref,) = rest

    @pl.when(step == 0)
    def _():
        acc_refs[1][...] = jnp.zeros(acc_refs[1].shape, F32)

    def body(acc_w, acc_r):
        for hh in range(heads):
            sl = slice(hh * HEAD_DIM, (hh + 1) * HEAD_DIM)
            xh = acc_r[:, sl]
            y = (xh * jnp.where(is_value, 1.0, _rms(xh, HEAD_DIM))) * gain_ref[:, sl]
            if rope:
                y = y * cos_ref[...] + pltpu.roll(y, HEAD_DIM // 2, 1) * sin_ref[...]
            o_ref[:, sl] = y.astype(o_ref.dtype)
        acc_w[...] = _dot(h_ref[...], w_ref[...])

    _by_parity(step, body, acc_refs)


def qkv_proj(h, w, qn, kn, n_q, n_k, seq, rope, out_dtype):
    t, d = h.shape
    n = w.shape[1]
    tm = _tile(seq, 1024)
    tn = _tile(math.gcd(n_q, n_k), 512)
    per = seq // tm
    n_norm = (n_q + n_k) // tn
    n_col = n // tn
    n_tiles = (t // tm) * n_col
    gains = jnp.concatenate([jnp.tile(qn.astype(F32), n_q // HEAD_DIM), jnp.tile(kn.astype(F32), n_k // HEAD_DIM),
                             jnp.ones((n - n_q - n_k,), F32)]).reshape(1, n)
    cur = lambda s: jnp.minimum(s, n_tiles - 1)
    lag = lambda s: jnp.maximum(s - 1, 0)
    in_specs = [
        pl.BlockSpec((tm, d), lambda s: (cur(s) // n_col, 0)),
        pl.BlockSpec((d, tn), lambda s: (0, cur(s) % n_col)),
        pl.BlockSpec((1, tn), lambda s: (0, lag(s) % n_col)),
    ]
    args = [h, w, gains]
    if rope is not None:
        cos, sin = rope
        tab = pl.BlockSpec((None, tm, HEAD_DIM),
                           lambda s: (jnp.where(lag(s) % n_col >= n_norm, 1, 0), (lag(s) // n_col) % per, 0))
        in_specs += [tab, tab]
        args += [jnp.stack([cos, jnp.ones_like(cos)]), jnp.stack([sin, jnp.zeros_like(sin)])]
    return pl.pallas_call(
        functools.partial(_qkv_kernel, n_norm=n_norm, n_col=n_col, rope=rope is not None,
                          heads=tn // HEAD_DIM),
        grid=(n_tiles + 1,),
        in_specs=in_specs,
        out_specs=pl.BlockSpec((tm, tn), lambda s: (lag(s) // n_col, lag(s) % n_col)),
        out_shape=jax.ShapeDtypeStruct((t, n), out_dtype),
        scratch_shapes=[pltpu.VMEM((tm, tn), F32), pltpu.VMEM((tm, tn), F32)],
        compiler_params=_params(1),
        name="qkv_proj",
    )(*args)


def rope_tables(seq):
    t = jnp.arange(seq)
    row = (t // GRID_W).astype(F32)
    col = (t % GRID_W).astype(F32)
    nf = HEAD_DIM // 4
    inv = ROPE_THETA ** (-jnp.arange(nf, dtype=F32) / nf)
    ang = jnp.concatenate([row[:, None] * inv, col[:, None] * inv], axis=-1)
    cos, sin = jnp.cos(ang), jnp.sin(ang)
    return jnp.concatenate([cos, cos], axis=-1), jnp.concatenate([-sin, sin], axis=-1)


ATTN_TQ = 512
DIFF_TQ = 256
ATTN_TK = 1024
PASS_UNROLL = 2


def _stack_heads(q, idx):
    parts = [q[:, i * HEAD_DIM:(i + 1) * HEAD_DIM] for i in idx]
    return parts[0] if len(parts) == 1 else jnp.concatenate(parts, axis=0)


def _lane_slabs(x):
    return [x[:, i * LANES:(i + 1) * LANES] for i in range(x.shape[1] // LANES)]


def _tile_lanes(x, width):
    n = width // LANES
    return x if n == 1 else jnp.concatenate([x] * n, axis=1)


def _tile_rows(x, n):
    return x if n == 1 else jnp.concatenate([x] * n, axis=0)


def _logits_pass(qs, kc, s_out, m_ref, bias=None):
    s = _dot_nt(qs, kc) * QK_SCALE2
    if bias is not None:
        s = s + bias
    s_out[...] = s
    m_ref[...] = jnp.maximum(m_ref[...], functools.reduce(jnp.maximum, _lane_slabs(s)))


def _finish_max(m_ref):
    m_ref[...] = jnp.broadcast_to(jnp.max(m_ref[...], axis=-1, keepdims=True), m_ref.shape)


def _with_ones(v):
    return jnp.concatenate([v, jnp.ones((v.shape[0], LANES), v.dtype)], axis=1)


def _value_pass(s_in, vc, m_ref, l_ref, acc_ref):
    s = s_in[...]
    p = jnp.exp2((s - _tile_lanes(m_ref[...], s.shape[1])).astype(BF16))
    if l_ref is None:
        acc_ref[...] += _dot(p, _with_ones(vc))
    else:
        l_ref[...] += functools.reduce(jnp.add, _lane_slabs(p.astype(F32)))
        acc_ref[...] += _dot(p, vc)


def _row_sum(l_ref):
    return jnp.sum(l_ref[...], axis=-1, keepdims=True)


def _sink_rows(sink_ref, groups, tq):
    sk = sink_ref[0]
    parts = [jnp.broadcast_to(sk[g:g + 1, :], (tq, LANES)) for g in range(groups)]
    return parts[0] if groups == 1 else jnp.concatenate(parts, axis=0)


def _attn_kernel(*refs, groups, tq, tk, seq, has_cache, has_sink):
    refs = list(refs)
    q_ref, k_ref, v_ref = refs[:3]
    pos = 3
    if has_cache:
        ck_ref, cv_ref = refs[pos:pos + 2]
        pos += 2
    if has_sink:
        sink_ref = refs[pos]
        pos += 1
    o_ref, s_ref = refs[pos:pos + 2]
    pos += 2
    if has_cache:
        sc_ref = refs[pos]
        pos += 1
    m_ref, acc_ref = refs[pos:]
    rows = groups * tq
    n_chunks = seq // tk

    qs = _stack_heads(q_ref[...].astype(BF16), range(groups))
    if has_sink:
        sink2 = _sink_rows(sink_ref, groups, tq)
        m_ref[...] = sink2
    else:
        m_ref[...] = jnp.full((rows, LANES), NEG_INF, F32)
    acc_ref[...] = jnp.zeros(acc_ref.shape, F32)

    def p1(c, carry):
        off = pl.multiple_of(c * tk, tk)
        _logits_pass(qs, k_ref[pl.ds(off, tk), :].astype(BF16), s_ref.at[c], m_ref)
        return carry
    lax.fori_loop(0, n_chunks, p1, 0, unroll=PASS_UNROLL)
    if has_cache:
        _logits_pass(qs, ck_ref[...].astype(BF16), sc_ref, m_ref)
    _finish_max(m_ref)

    def p2(c, carry):
        off = pl.multiple_of(c * tk, tk)
        _value_pass(s_ref.at[c], v_ref[pl.ds(off, tk), :].astype(BF16), m_ref, None, acc_ref)
        return carry
    lax.fori_loop(0, n_chunks, p2, 0, unroll=PASS_UNROLL)
    if has_cache:
        _value_pass(sc_ref, cv_ref[...].astype(BF16), m_ref, None, acc_ref)

    acc = acc_ref[...]
    l = acc[:, HEAD_DIM:]
    if has_sink:
        l = l + jnp.exp2(sink2 - m_ref[...])
    o = acc[:, :HEAD_DIM] / l
    for g in range(groups):
        o_ref[:, g * HEAD_DIM:(g + 1) * HEAD_DIM] = o[g * tq:(g + 1) * tq].astype(o_ref.dtype)


def _sink_arg(sink, n_kv, groups):
    sink2 = sink.astype(F32).reshape(n_kv, groups, 1) * LOG2E
    return jnp.broadcast_to(sink2, (n_kv, groups, LANES))


def attention(qkv, n_heads, n_kv, cache=None, sink=None):
    b, seq, _ = qkv.shape
    groups = n_heads // n_kv
    tq = _tile(seq, ATTN_TQ)
    tk = _tile(seq, ATTN_TK)
    gw = groups * HEAD_DIM
    rows = groups * tq
    in_specs = [
        pl.BlockSpec((None, tq, gw), lambda bi, h, qi: (bi, qi, h)),
        pl.BlockSpec((None, seq, HEAD_DIM), lambda bi, h, qi: (bi, 0, n_heads + h)),
        pl.BlockSpec((None, seq, HEAD_DIM), lambda bi, h, qi: (bi, 0, n_heads + n_kv + h)),
    ]
    args = [qkv, qkv, qkv]
    scratch = [pltpu.VMEM((seq // tk, rows, tk), F32)]
    if cache is not None:
        ck, cv = cache
        past = ck.shape[1]
        spec = pl.BlockSpec((None, past, HEAD_DIM), lambda bi, h, qi: (bi, 0, h))
        in_specs += [spec, spec]
        args += [ck.reshape(b, past, n_kv * HEAD_DIM), cv.reshape(b, past, n_kv * HEAD_DIM)]
        scratch.append(pltpu.VMEM((rows, past), F32))
    if sink is not None:
        in_specs.append(pl.BlockSpec((1, groups, LANES), lambda bi, h, qi: (h, 0, 0)))
        args.append(_sink_arg(sink, n_kv, groups))
    scratch += [pltpu.VMEM((rows, LANES), F32), pltpu.VMEM((rows, HEAD_DIM + LANES), F32)]
    return pl.pallas_call(
        functools.partial(_attn_kernel, groups=groups, tq=tq, tk=tk, seq=seq,
                          has_cache=cache is not None, has_sink=sink is not None),
        grid=(b, n_kv, seq // tq),
        in_specs=in_specs,
        out_specs=pl.BlockSpec((None, tq, gw), lambda bi, h, qi: (bi, qi, h)),
        out_shape=jax.ShapeDtypeStruct((b, seq, n_heads * HEAD_DIM), BF16),
        scratch_shapes=scratch,
        compiler_params=_params(3),
        name="attention",
    )(*args)


WIN_QB = 128


def _window_masks(seq):
    band = WIN_QB + 2 * WINDOW
    out = []
    for q0 in (0, WIN_QB, seq - WIN_QB):
        start = int(np.clip(q0 - WINDOW, 0, seq - band))
        qpos = q0 + np.arange(WIN_QB)[:, None]
        kpos = start + np.arange(band)[None, :]
        out.append(np.where(np.abs(qpos - kpos) <= WINDOW, 0.0, NEG_INF))
    return jnp.asarray(np.stack(out), F32)


def _win_kernel(q_ref, k_ref, v_ref, ck_ref, cv_ref, sink_ref, wmask_ref, o_ref, *, groups, seq):
    tq = WIN_QB
    band = tq + 2 * WINDOW
    nb = seq // tq
    past = ck_ref.shape[0]
    kctx = ck_ref[...].astype(BF16)
    vctx1 = _with_ones(cv_ref[...].astype(BF16))
    sink2 = _sink_rows(sink_ref, groups, tq)

    def body(j, carry):
        q0 = pl.multiple_of(j * tq, tq)
        qs = _stack_heads(q_ref[pl.ds(q0, tq), :], range(groups))
        start = pl.multiple_of(jnp.clip(q0 - WINDOW, 0, seq - band), WINDOW)
        case = jnp.where(j == 0, 0, jnp.where(j == nb - 1, 2, 1))
        s1 = _dot_nt(qs, k_ref[pl.ds(start, band), :]) * QK_SCALE2 + _tile_rows(wmask_ref[case], groups)
        s2 = _dot_nt(qs, kctx) * QK_SCALE2
        mp = functools.reduce(jnp.maximum, _lane_slabs(s1) + _lane_slabs(s2) + [sink2])
        m = jnp.broadcast_to(jnp.max(mp, axis=-1, keepdims=True), mp.shape)
        p1 = jnp.exp2((s1 - _tile_lanes(m, band)).astype(BF16))
        p2 = jnp.exp2((s2 - _tile_lanes(m, past)).astype(BF16))
        acc = _dot(p1, _with_ones(v_ref[pl.ds(start, band), :])) + _dot(p2, vctx1)
        o = acc[:, :HEAD_DIM] / (acc[:, HEAD_DIM:] + jnp.exp2(sink2 - m))
        for g in range(groups):
            o_ref[pl.ds(q0, tq), g * HEAD_DIM:(g + 1) * HEAD_DIM] = o[g * tq:(g + 1) * tq].astype(o_ref.dtype)
        return carry
    lax.fori_loop(0, nb, body, 0, unroll=4)


def window_attention(qkv, n_heads, n_kv, cache, sink):
    b, seq, _ = qkv.shape
    groups = n_heads // n_kv
    gw = groups * HEAD_DIM
    band = WIN_QB + 2 * WINDOW
    assert seq % (2 * WIN_QB) == 0 and seq >= band + WIN_QB and qkv.dtype == BF16
    ck, cv = cache
    past = ck.shape[1]
    cspec = pl.BlockSpec((None, past, HEAD_DIM), lambda bi, h: (bi, 0, h))
    return pl.pallas_call(
        functools.partial(_win_kernel, groups=groups, seq=seq),
        grid=(b, n_kv),
        in_specs=[
            pl.BlockSpec((None, seq, gw), lambda bi, h: (bi, 0, h)),
            pl.BlockSpec((None, seq, HEAD_DIM), lambda bi, h: (bi, 0, n_heads + h)),
            pl.BlockSpec((None, seq, HEAD_DIM), lambda bi, h: (bi, 0, n_heads + n_kv + h)),
            cspec, cspec,
            pl.BlockSpec((1, groups, LANES), lambda bi, h: (h, 0, 0)),
            pl.BlockSpec((3, WIN_QB, band), lambda bi, h: (0, 0, 0)),
        ],
        out_specs=pl.BlockSpec((None, seq, gw), lambda bi, h: (bi, 0, h)),
        out_shape=jax.ShapeDtypeStruct((b, seq, n_heads * HEAD_DIM), BF16),
        compiler_params=_params(2),
        name="window_attention",
    )(qkv, qkv, qkv, ck.reshape(b, past, n_kv * HEAD_DIM), cv.reshape(b, past, n_kv * HEAD_DIM),
      _sink_arg(sink, n_kv, groups), _window_masks(seq))


def _diff_kernel(*refs, groups, tq, tk, seq, has_cache, lam_init):
    refs = list(refs)
    q_ref, k_ref, v_ref = refs[:3]
    pos = 3
    if has_cache:
        ck_ref, cv_ref = refs[pos:pos + 2]
        pos += 2
    lq1_ref, lk1_ref, lq2_ref, lk2_ref, sub_ref = refs[pos:pos + 5]
    pos += 5
    o_ref, s_ref = refs[pos:pos + 2]
    pos += 2
    if has_cache:
        sc_ref = refs[pos]
        pos += 1
    m_ref, l_ref, acc_ref = refs[pos:]
    rows = groups * tq
    dv = 2 * HEAD_DIM
    n_chunks = seq // tk

    q = q_ref[...].astype(BF16)
    outs = []
    for i in range(2):
        cols = slice(i * HEAD_DIM, (i + 1) * HEAD_DIM)
        qs = _stack_heads(q, [2 * g + i for g in range(groups)])
        m_ref[...] = jnp.full((rows, LANES), NEG_INF, F32)
        l_ref[...] = jnp.zeros((rows, LANES), F32)
        acc_ref[...] = jnp.zeros((rows, dv), F32)

        def p1(c, carry, qs=qs, cols=cols):
            off = pl.multiple_of(c * tk, tk)
            _logits_pass(qs, k_ref[pl.ds(off, tk), cols].astype(BF16), s_ref.at[c], m_ref)
            return carry
        lax.fori_loop(0, n_chunks, p1, 0, unroll=PASS_UNROLL)
        if has_cache:
            _logits_pass(qs, ck_ref[:, cols].astype(BF16), sc_ref, m_ref)
        _finish_max(m_ref)

        def p2(c, carry):
            off = pl.multiple_of(c * tk, tk)
            _value_pass(s_ref.at[c], v_ref[pl.ds(off, tk), :].astype(BF16), m_ref, l_ref, acc_ref)
            return carry
        lax.fori_loop(0, n_chunks, p2, 0, unroll=PASS_UNROLL)
        if has_cache:
            _value_pass(sc_ref, cv_ref[...].astype(BF16), m_ref, l_ref, acc_ref)
        outs.append(acc_ref[...] / _row_sum(l_ref))

    e1 = jnp.exp(jnp.sum(lq1_ref[...] * lk1_ref[...], axis=-1, keepdims=True))
    e2 = jnp.exp(jnp.sum(lq2_ref[...] * lk2_ref[...], axis=-1, keepdims=True))
    lam = e1 - e2 + lam_init
    o = outs[0] - lam * outs[1]
    o = ((o * _rms(o, dv)) * sub_ref[...]) * (1.0 - lam_init)
    for g in range(groups):
        o_ref[:, g * dv:(g + 1) * dv] = o[g * tq:(g + 1) * tq].astype(o_ref.dtype)


def diff_attention(qkv, n_heads, n_kv, lam_vecs, subln, lam_init, cache=None):
    b, seq, _ = qkv.shape
    groups = n_heads // n_kv
    dv = 2 * HEAD_DIM
    tq = _tile(seq, DIFF_TQ)
    tk = _tile(seq, ATTN_TK)
    gw = groups * dv
    rows = groups * tq
    in_specs = [
        pl.BlockSpec((None, tq, gw), lambda bi, h, qi: (bi, qi, h)),
        pl.BlockSpec((None, seq, dv), lambda bi, h, qi: (bi, 0, n_heads + h)),
        pl.BlockSpec((None, seq, dv), lambda bi, h, qi: (bi, 0, n_heads + n_kv + h)),
    ]
    args = [qkv, qkv, qkv]
    scratch = [pltpu.VMEM((seq // tk, rows, tk), F32)]
    if cache is not None:
        ck, cv = cache
        past = ck.shape[1]
        spec = pl.BlockSpec((None, past, dv), lambda bi, h, qi: (bi, 0, h))
        in_specs += [spec, spec]
        args += [ck.reshape(b, past, n_kv * dv), cv.reshape(b, past, n_kv * dv)]
        scratch.append(pltpu.VMEM((rows, past), F32))
    vec = pl.BlockSpec((1, HEAD_DIM), lambda bi, h, qi: (0, 0))
    in_specs += [vec] * 4 + [pl.BlockSpec((1, dv), lambda bi, h, qi: (0, 0))]
    args += [v.astype(F32).reshape(1, HEAD_DIM) for v in lam_vecs] + [subln.astype(F32).reshape(1, dv)]
    scratch += [pltpu.VMEM((rows, LANES), F32), pltpu.VMEM((rows, LANES), F32),
                pltpu.VMEM((rows, dv), F32)]
    return pl.pallas_call(
        functools.partial(_diff_kernel, groups=groups, tq=tq, tk=tk, seq=seq,
                          has_cache=cache is not None, lam_init=lam_init),
        grid=(b, n_kv, seq // tq),
        in_specs=in_specs,
        out_specs=pl.BlockSpec((None, tq, gw), lambda bi, h, qi: (bi, qi, h)),
        out_shape=jax.ShapeDtypeStruct((b, seq, n_heads * dv), BF16),
        scratch_shapes=scratch,
        compiler_params=_params(3),
        name="diff_attention",
    )(*args)


NAT_QB = 256


def _nat_geometry(seq):
    rows = seq // GRID_W
    kr_n = min(NA_ROWS, rows)
    q_rows = NAT_QB // GRID_W
    span = min(kr_n + q_rows - 1, rows)
    return rows, kr_n, q_rows, span


def _nat_band_start(j, seq):
    rows, kr_n, q_rows, span = _nat_geometry(seq)
    clip = jnp.clip if isinstance(j, jax.Array) else np.clip
    mini = jnp.minimum if isinstance(j, jax.Array) else np.minimum
    return mini(clip(j * q_rows - kr_n // 2, 0, rows - kr_n), rows - span)


def _nat_block_pattern(j, seq):
    rows, kr_n, q_rows, span = _nat_geometry(seq)
    qr = j * q_rows + np.arange(q_rows)
    rs = np.clip(qr - kr_n // 2, 0, rows - kr_n)
    kr = int(_nat_band_start(j, seq)) + np.arange(span)
    ok_r = (kr[None, :] >= rs[:, None]) & (kr[None, :] < rs[:, None] + kr_n)
    dr = np.clip(kr[None, :] - qr[:, None] + NA_ROWS - 1, 0, 2 * NA_ROWS - 2)
    return ok_r, dr


def _nat_classes(seq):
    nb = seq // NAT_QB
    reps, change = [], []
    for j in range(nb):
        ok_r, dr = _nat_block_pattern(j, seq)
        key = (ok_r.tobytes(), np.where(ok_r, dr, 0).tobytes())
        if not reps or key != reps[-1][0]:
            assert all(key != k for k, _ in reps), "window patterns must be contiguous runs of blocks"
            reps.append((key, j))
            change.append(j)
    return [j for _, j in reps], change[1:]


def nat_bias_tables(rpb, seq):
    _, _, q_rows, span = _nat_geometry(seq)
    n_heads = rpb.shape[0]
    cl = np.arange(GRID_W)
    cs = np.clip(cl - NA_COLS // 2, 0, GRID_W - NA_COLS)
    ok_c = (cl[None, :] >= cs[:, None]) & (cl[None, :] < cs[:, None] + NA_COLS)
    dc = np.clip(cl[None, :] - cl[:, None] + NA_COLS - 1, 0, 2 * NA_COLS - 2)
    by_col = rpb.astype(F32)[:, :, dc] * LOG2E
    reps, _ = _nat_classes(seq)
    tabs = []
    for j in reps:
        ok_r, dr = _nat_block_pattern(j, seq)
        t = by_col[:, dr]
        ok = ok_r[:, :, None, None] & ok_c[None, None]
        t = jnp.where(ok[None], t, NEG_INF)
        tabs.append(t.transpose(0, 1, 3, 2, 4).reshape(n_heads, q_rows * GRID_W, span * GRID_W))
    return jnp.stack(tabs, axis=0)


def _nat_kernel(q_ref, k_ref, v_ref, ck_ref, cv_ref, t_ref, o_ref, *, seq, change):
    nb = seq // NAT_QB
    band = t_ref.shape[-1]
    kctx = ck_ref[...].astype(BF16)
    vctx1 = _with_ones(cv_ref[...].astype(BF16))

    def body(j, carry):
        q0 = pl.multiple_of(j * NAT_QB, NAT_QB)
        q = q_ref[pl.ds(q0, NAT_QB), :]
        k0 = pl.multiple_of(_nat_band_start(j, seq) * GRID_W, GRID_W)
        cls = functools.reduce(lambda a, cp: a + jnp.where(j >= cp, 1, 0), change, 0)
        s1 = _dot_nt(q, k_ref[pl.ds(k0, band), :]) * QK_SCALE2 + t_ref[cls, 0]
        s2 = _dot_nt(q, kctx) * QK_SCALE2
        m = jnp.maximum(jnp.max(s1, axis=-1, keepdims=True), jnp.max(s2, axis=-1, keepdims=True))
        p1 = jnp.exp2((s1 - m).astype(BF16))
        p2 = jnp.exp2((s2 - m).astype(BF16))
        acc = _dot(p1, _with_ones(v_ref[pl.ds(k0, band), :])) + _dot(p2, vctx1)
        o = acc[:, :HEAD_DIM] / acc[:, HEAD_DIM:]
        o_ref[pl.ds(q0, NAT_QB), :] = o.astype(o_ref.dtype)
        return carry
    lax.fori_loop(0, nb, body, 0, unroll=4)


def nat_attention(qkv, n_heads, cache, tables):
    b, seq, _ = qkv.shape
    ck, cv = cache
    past = ck.shape[1]
    ncls, _, _, band = tables.shape
    _, change = _nat_classes(seq)
    blk = lambda off: pl.BlockSpec((None, seq, HEAD_DIM), lambda bi, h: (bi, 0, off + h))
    cspec = pl.BlockSpec((None, past, HEAD_DIM), lambda bi, h: (bi, 0, h))
    return pl.pallas_call(
        functools.partial(_nat_kernel, seq=seq, change=tuple(change)),
        grid=(b, n_heads),
        in_specs=[blk(0), blk(n_heads), blk(2 * n_heads), cspec, cspec,
                  pl.BlockSpec((ncls, 1, NAT_QB, band), lambda bi, h: (0, h, 0, 0))],
        out_specs=pl.BlockSpec((None, seq, HEAD_DIM), lambda bi, h: (bi, 0, h)),
        out_shape=jax.ShapeDtypeStruct((b, seq, n_heads * HEAD_DIM), BF16),
        compiler_params=_params(2),
        name="nat_attention",
    )(qkv, qkv, qkv, ck.reshape(b, past, n_heads * HEAD_DIM), cv.reshape(b, past, n_heads * HEAD_DIM), tables)


def kernel(x_prompt, x_sample, cache_k0, cache_v0, cache_k1, cache_v1, cache_k2, cache_v2, cache_k3, cache_v3, c, c_ctx, norm_g, w_ada, b_ada, w_ffn_in, w_ffn_out, att_wqkv, att_wo, att_qn, att_kn, win_wqkv, win_wo, win_qn, win_kn, win_sink, diff_wqkv, diff_wo, diff_qn, diff_kn, diff_lq1, diff_lk1, diff_lq2, diff_lk2, diff_subln, nat_wqkv, nat_wo, nat_qn, nat_kn, nat_rpb):
    depth = w_ada.shape[0]
    d = x_prompt.shape[-1]
    bp, sp, _ = x_prompt.shape
    bs, ss, _ = x_sample.shape
    caches = [(cache_k0, cache_v0), (cache_k1, cache_v1), (cache_k2, cache_v2), (cache_k3, cache_v3)]
    kv_heads = [cache_k0.shape[2], cache_k1.shape[2], cache_k2.shape[2], cache_k3.shape[2]]
    wqkv = [w.astype(BF16) for w in (att_wqkv, win_wqkv, diff_wqkv, nat_wqkv)]
    wo = [w.astype(BF16) for w in (att_wo, win_wo, diff_wo, nat_wo)]
    q_cols = [w.shape[0] for w in wo]
    qn = [att_qn, win_qn, diff_qn, nat_qn]
    kn = [att_kn, win_kn, diff_kn, nat_kn]
    w_in = w_ffn_in.astype(BF16)
    w_out = w_ffn_out.astype(BF16)
    lam_vecs = (diff_lq1, diff_lk1, diff_lq2, diff_lk2)

    rows = 8 * (-(-(1 + bs) // 8))
    cond = jnp.zeros((rows, d), F32).at[0].set(c_ctx).at[1:1 + bs].set(c)
    mods = adaln(cond, w_ada, b_ada).reshape(depth, rows, N_MOD, d)

    rope = rope_tables(ss)
    tables = nat_bias_tables(nat_rpb, ss)

    def run(x3, latent):
        b, seq, _ = x3.shape
        mseq = seq if latent else b * seq

        def mod(l, chunk):
            r = mods[l, 1:1 + bs, chunk] if latent else mods[l, 0:1, chunk]
            return r[:, None, :]

        x = x3.reshape(b * seq, d)
        h = normmod(x, norm_g[0, 0], mod(0, 0), mod(0, 1), mseq)
        state = []
        for l in range(depth):
            kind = l % 4
            n_kv = kv_heads[kind]
            a = ffn_in(h, w_in, (l, 0))
            x, h = resid_norm(a, w_out, (l, 0), x, mod(l, 2), 0.5, mseq,
                              norm=(norm_g[l, 1], mod(l, 3), mod(l, 4)))
            kv_cols = (wqkv[kind].shape[1] - q_cols[kind]) // 2
            use_rope = latent and kind != 3
            qkv = qkv_proj(h, wqkv[kind], qn[kind], kn[kind], q_cols[kind], kv_cols,
                           seq if use_rope else mseq, rope if use_rope else None,
                           BF16 if latent else F32)
            qkv = qkv.reshape(b, seq, -1)
            cache = caches[kind] if latent else None
            if kind == 2:
                lam_init = 0.8 - 0.6 * math.exp(-0.3 * l)
                n_heads = q_cols[kind] // (2 * HEAD_DIM)
                o = diff_attention(qkv, n_heads, n_kv, lam_vecs, diff_subln, lam_init, cache)
            elif kind == 3 and latent:
                n_heads = q_cols[kind] // HEAD_DIM
                o = nat_attention(qkv, n_heads, cache, tables)
            elif kind == 1 and latent:
                n_heads = q_cols[kind] // HEAD_DIM
                o = window_attention(qkv, n_heads, n_kv, cache, win_sink)
            else:
                n_heads = q_cols[kind] // HEAD_DIM
                o = attention(qkv, n_heads, n_kv, cache, sink=win_sink if kind == 1 else None)
            if not latent:
                nq = q_cols[kind]
                state.append(qkv[:, :, nq:nq + kv_cols])
                state.append(qkv[:, :, nq + kv_cols:])
            x, h = resid_norm(o.reshape(b * seq, -1), wo[kind][None, None], (0, 0), x, mod(l, 5), 1.0, mseq,
                              norm=(norm_g[l, 2], mod(l, 6), mod(l, 7)))
            a = ffn_in(h, w_in, (l, 1))
            if l + 1 < depth:
                x, h = resid_norm(a, w_out, (l, 1), x, mod(l, 8), 0.5, mseq,
                                  norm=(norm_g[l + 1, 0], mod(l + 1, 0), mod(l + 1, 1)))
            else:
                x = resid_norm(a, w_out, (l, 1), x, mod(l, 8), 0.5, mseq)
        return x.reshape(b, seq, d), state

    yp, st = run(x_prompt, False)
    ys, _ = run(x_sample, True)
    hd = HEAD_DIM
    new_state = [
        st[0].reshape(bp, sp, kv_heads[0], hd), st[1].reshape(bp, sp, kv_heads[0], hd),
        st[2].reshape(bp, sp, kv_heads[1], hd), st[3].reshape(bp, sp, kv_heads[1], hd),
        st[4].reshape(bp, sp, kv_heads[2], 2, hd), st[5].reshape(bp, sp, kv_heads[2], 2 * hd),
        st[6].reshape(bp, sp, kv_heads[3], hd), st[7].reshape(bp, sp, kv_heads[3], hd),
    ]
    return (yp, ys, *new_state)
```
